```python
import math
import jax
import jax.numpy as jnp
from jax import lax
import numpy as np

D_MODEL = 1024
BATCH = 8
SEQ = 4096
DEPTH = 4

CTX_LEN = 256
GRID_W = 64
N_MIXERS = 3
N_MOD = 9
MACARON_WEIGHT = 0.5
D_FF = 256 * ((8 * D_MODEL // 3 + 255) // 256)
NORM_EPS = 1e-6
ROPE_THETA = 10000.0
Q_BLOCK = 128

DA_HEAD_DIM = 64
DA_HEADS = D_MODEL // (2 * DA_HEAD_DIM)
DA_WIDTH = DA_HEADS * 2 * DA_HEAD_DIM

HG_KEY = 128
HG_VAL = 128
HG_HEADS = D_MODEL // HG_VAL
HG_KW = HG_HEADS * HG_KEY
HG_VW = HG_HEADS * HG_VAL
HG_CHUNK = 64

GQA_HEAD_DIM = 128
GQA_Q_HEADS = D_MODEL // GQA_HEAD_DIM
GQA_KV_HEADS = GQA_Q_HEADS // 4
GQA_GROUP = GQA_Q_HEADS // GQA_KV_HEADS
GQA_Q_W = GQA_Q_HEADS * GQA_HEAD_DIM
GQA_KV_W = GQA_KV_HEADS * GQA_HEAD_DIM

N_DA = (DEPTH + N_MIXERS - 1) // N_MIXERS
N_HG = (DEPTH + N_MIXERS - 2) // N_MIXERS
N_GQA = (DEPTH + N_MIXERS - 3) // N_MIXERS

kernel_name = "hybrid_interleaved_dit_block"


def rmsnorm(x, g):
    xf = x.astype(jnp.float32)
    y = xf * lax.rsqrt(jnp.mean(xf * xf, axis=-1, keepdims=True) + NORM_EPS)
    return (y * g.astype(jnp.float32)).astype(x.dtype)


def modulate(x, g_pre, shift, scale):
    return rmsnorm(x, g_pre) * (1 + scale) + shift


def gated_residual(x, y, g_post, gate, weight):
    return x + weight * gate * rmsnorm(y, g_post)


def swiglu(h, w_in, w_out):
    gate, up = jnp.split(h @ w_in, 2, axis=-1)
    return (jax.nn.silu(gate) * up) @ w_out


def ffn_sublayer(x, shift, scale, gate, g_pre, g_post, w_in, w_out):
    y = swiglu(modulate(x, g_pre, shift, scale), w_in, w_out)
    return gated_residual(x, y, g_post, gate, MACARON_WEIGHT)


def axial_rope_tables(rows, head_dim):
    pairs = head_dim // 4
    inv_freq = jnp.power(ROPE_THETA, -jnp.arange(pairs, dtype=jnp.float32) / pairs)
    r = jnp.repeat(jnp.arange(rows, dtype=jnp.float32), GRID_W)
    col = jnp.tile(jnp.arange(GRID_W, dtype=jnp.float32), rows)
    ang = jnp.concatenate([r[:, None] * inv_freq, col[:, None] * inv_freq], axis=-1)
    return jnp.cos(ang), jnp.sin(ang)


def apply_rope(x, cos, sin):
    n, half = cos.shape
    shape = (1, n) + (1,) * (x.ndim - 3) + (half,)
    cs = cos.reshape(shape).astype(x.dtype)
    sn = sin.reshape(shape).astype(x.dtype)
    x1, x2 = x[..., 0::2], x[..., 1::2]
    return jnp.stack([x1 * cs - x2 * sn, x1 * sn + x2 * cs], axis=-1).reshape(x.shape)


def sweep_query_blocks(fn, q):
    bsz, n = q.shape[:2]
    qb = jnp.moveaxis(q.reshape((bsz, n // Q_BLOCK, Q_BLOCK) + q.shape[2:]), 1, 0)
    ob = lax.map(fn, qb)
    return jnp.moveaxis(ob, 0, 1).reshape((bsz, n) + ob.shape[3:])


def differential_attention(h_l, h_c, w_qkv, lam_p, subln, w_o, lam_init, cos, sin, need_ctx):
    bsz = h_l.shape[0]

    def project(h):
        n = h.shape[1]
        q, k, v = jnp.split(h @ w_qkv, 3, axis=-1)
        return (q.reshape(bsz, n, DA_HEADS, 2, DA_HEAD_DIM),
                k.reshape(bsz, n, DA_HEADS, 2, DA_HEAD_DIM),
                v.reshape(bsz, n, DA_HEADS, 2 * DA_HEAD_DIM))

    q_l, k_l, v_l = project(h_l)
    q_c, k_c, v_c = project(h_c)
    q_l = apply_rope(q_l, cos, sin)
    k_l = apply_rope(k_l, cos, sin)
    lp = lam_p.astype(jnp.float32)
    lam = jnp.exp(jnp.sum(lp[0] * lp[1])) - jnp.exp(jnp.sum(lp[2] * lp[3])) + lam_init
    scale = DA_HEAD_DIM ** -0.5

    def attend(q, k, v):
        s = jnp.einsum('bqhmd,bkhmd->bhmqk', q, k).astype(jnp.float32) * scale
        p = jax.nn.softmax(s, axis=-1)
        a = (p[:, :, 0] - lam * p[:, :, 1]).astype(v.dtype)
        return jnp.einsum('bhqk,bkhe->bqhe', a, v)

    def finish(o):
        o = rmsnorm(o, subln) * (1 - lam_init)
        return o.reshape(bsz, o.shape[1], DA_WIDTH) @ w_o

    k_all = jnp.concatenate([k_l, k_c], axis=1)
    v_all = jnp.concatenate([v_l, v_c], axis=1)
    y_l = finish(sweep_query_blocks(lambda qb: attend(qb, k_all, v_all), q_l))
    y_c = finish(attend(q_c, k_c, v_c)) if need_ctx else None
    return y_l, y_c


def gqa_attention(h_l, h_c, w_qkv, q_norm, k_norm, w_o, cos, sin, need_ctx):
    bsz = h_l.shape[0]

    def project(h):
        n = h.shape[1]
        q, k, v = jnp.split(h @ w_qkv, [GQA_Q_W, GQA_Q_W + GQA_KV_W], axis=-1)
        q = rmsnorm(q.reshape(bsz, n, GQA_KV_HEADS, GQA_GROUP, GQA_HEAD_DIM), q_norm)
        k = rmsnorm(k.reshape(bsz, n, GQA_KV_HEADS, GQA_HEAD_DIM), k_norm)
        return q, k, v.reshape(bsz, n, GQA_KV_HEADS, GQA_HEAD_DIM)

    q_l, k_l, v_l = project(h_l)
    q_c, k_c, v_c = project(h_c)
    q_l = apply_rope(q_l, cos, sin)
    k_l = apply_rope(k_l, cos, sin)
    scale = GQA_HEAD_DIM ** -0.5

    def attend(q, k, v):
        s = jnp.einsum('bqhgd,bkhd->bhgqk', q, k).astype(jnp.float32) * scale
        p = jax.nn.softmax(s, axis=-1).astype(v.dtype)
        return jnp.einsum('bhgqk,bkhd->bqhgd', p, v)

    def finish(o):
        return o.reshape(bsz, o.shape[1], GQA_Q_W) @ w_o

    k_all = jnp.concatenate([k_l, k_c], axis=1)
    v_all = jnp.concatenate([v_l, v_c], axis=1)
    y_l = finish(sweep_query_blocks(lambda qb: attend(qb, k_all, v_all), q_l))
    y_c = finish(attend(q_c, k_c, v_c)) if need_ctx else None
    return y_l, y_c


def chunkwise_gated_scan(q, k, v, log_f, s0):
    bsz, n, heads, _ = q.shape
    nc = n // HG_CHUNK

    def to_chunks(a):
        return jnp.moveaxis(a.reshape(bsz, nc, HG_CHUNK, heads, a.shape[-1]), 1, 0)

    lower_tri = jnp.tril(jnp.ones((HG_CHUNK, HG_CHUNK), dtype=bool))[None, :, :, None, None]

    def step(state, chunk):
        qc, kc, vc, gc = chunk
        G = jnp.cumsum(gc, axis=1)
        g_last = G[:, -1]
        o_inter = jnp.einsum('bthk,bhkv->bthv', qc * jnp.exp(G), state)
        rel = jnp.exp(jnp.where(lower_tri, G[:, :, None] - G[:, None, :], -jnp.inf))
        att = jnp.einsum('bthk,btshk,bshk->bhts', qc, rel, kc)
        o_intra = jnp.einsum('bhts,bshv->bthv', att, vc)
        state = (jnp.exp(g_last)[..., None] * state
                 + jnp.einsum('bshk,bshv->bhkv', kc * jnp.exp(g_last[:, None] - G), vc))
        return state, o_inter + o_intra

    s_fin, o = lax.scan(step, s0, (to_chunks(q), to_chunks(k), to_chunks(v), to_chunks(log_f)))
    return jnp.moveaxis(o, 0, 1).reshape(bsz, n, heads, v.shape[-1]), s_fin


def hgrn2_bidirectional(h_l, h_c, w_in, lb_fwd, lb_bwd, norm_g, w_o, need_ctx):
    bsz = h_l.shape[0]
    lbs = (lb_fwd.reshape(HG_HEADS, HG_KEY), lb_bwd.reshape(HG_HEADS, HG_KEY))

    def project(h):
        n = h.shape[1]
        q, f_fwd, f_bwd, i_in, gate = jnp.split(
            h @ w_in, [HG_KW, 2 * HG_KW, 3 * HG_KW, 3 * HG_KW + HG_VW], axis=-1)

        def heads(a, d):
            return a.reshape(bsz, n, HG_HEADS, d).astype(jnp.float32)

        q = jax.nn.silu(heads(q, HG_KEY))
        gates = []
        for f_logit, lb in zip((f_fwd, f_bwd), lbs):
            f = lb + (1 - lb) * jax.nn.sigmoid(heads(f_logit, HG_KEY))
            gates.append((1 - f, jnp.log(f)))
        return q, gates, heads(i_in, HG_VAL), gate

    q_l, gates_l, v_l, gate_l = project(h_l)
    q_c, gates_c, v_c, gate_c = project(h_c)
    (k_cf, lf_cf), (k_cb, lf_cb) = gates_c
    (k_lf, lf_lf), (k_lb, lf_lb) = gates_l

    def flip(a):
        return a[:, ::-1]

    s0 = jnp.zeros((bsz, HG_HEADS, HG_KEY, HG_VAL), jnp.float32)
    o_cf, s_cf = chunkwise_gated_scan(q_c, k_cf, v_c, lf_cf, s0)
    o_cb, s_cb = chunkwise_gated_scan(flip(q_c), flip(k_cb), flip(v_c), flip(lf_cb), s0)
    o_lf, _ = chunkwise_gated_scan(q_l, k_lf, v_l, lf_lf, s_cf)
    o_lb, _ = chunkwise_gated_scan(flip(q_l), flip(k_lb), flip(v_l), flip(lf_lb), s_cb)

    def finish(o, gate):
        n = o.shape[1]
        o = rmsnorm(o, norm_g).reshape(bsz, n, HG_VW).astype(gate.dtype) * jax.nn.silu(gate)
        return o @ w_o

    y_l = finish(o_lf + flip(o_lb), gate_l)
    y_c = finish(o_cf + flip(o_cb), gate_c) if need_ctx else None
    return y_l, y_c


def setup_inputs(seed: int = 0) -> dict:
    key = jax.random.key(seed)
    ks = jax.random.split(key, 24)

    def nrm(k, shape, scale):
        return scale * jax.random.normal(k, shape, jnp.float32)

    def gain(k, shape):
        return 1.0 + 0.02 * jax.random.normal(k, shape, jnp.float32)

    return {
        "x": nrm(ks[0], (BATCH, SEQ, D_MODEL), 1.0),
        "c": nrm(ks[1], (BATCH, D_MODEL), 1.0),
        "ctx": nrm(ks[2], (BATCH, CTX_LEN, D_MODEL), 1.0),
        "c_ctx": nrm(ks[3], (D_MODEL,), 1.0),
        "w_mod": nrm(ks[4], (DEPTH, D_MODEL, N_MOD * D_MODEL), 0.5 * D_MODEL ** -0.5),
        "b_mod": nrm(ks[5], (DEPTH, N_MOD * D_MODEL), 0.01),
        "norm_g": gain(ks[6], (DEPTH, 6, D_MODEL)),
        "ffn_w_in": nrm(ks[7], (DEPTH, 2, D_MODEL, 2 * D_FF), D_MODEL ** -0.5),
        "ffn_w_out": nrm(ks[8], (DEPTH, 2, D_FF, D_MODEL), D_FF ** -0.5),
        "da_w_qkv": nrm(ks[9], (N_DA, D_MODEL, 3 * DA_WIDTH), D_MODEL ** -0.5),
        "da_lambda": nrm(ks[10], (N_DA, 4, DA_HEAD_DIM), 0.1),
        "da_subln": gain(ks[11], (N_DA, 2 * DA_HEAD_DIM)),
        "da_w_o": nrm(ks[12], (N_DA, DA_WIDTH, D_MODEL), DA_WIDTH ** -0.5),
        "hg_w_in": nrm(ks[13], (N_HG, D_MODEL, 3 * HG_KW + 2 * HG_VW), D_MODEL ** -0.5),
        "hg_lower_bound": nrm(ks[14], (2, DEPTH, HG_KW), 0.5),
        "hg_norm": gain(ks[15], (N_HG, HG_VAL)),
        "hg_w_o": nrm(ks[16], (N_HG, HG_VW, D_MODEL), HG_VW ** -0.5),
        "gqa_w_qkv": nrm(ks[17], (N_GQA, D_MODEL, GQA_Q_W + 2 * GQA_KV_W), D_MODEL ** -0.5),
        "gqa_q_norm": gain(ks[18], (N_GQA, GQA_HEAD_DIM)),
        "gqa_k_norm": gain(ks[19], (N_GQA, GQA_HEAD_DIM)),
        "gqa_w_o": nrm(ks[20], (N_GQA, GQA_Q_W, D_MODEL), GQA_Q_W ** -0.5),
    }


def reference(x, c, ctx, c_ctx, w_mod, b_mod, norm_g, ffn_w_in, ffn_w_out,
              da_w_qkv, da_lambda, da_subln, da_w_o,
              hg_w_in, hg_lower_bound, hg_norm, hg_w_o,
              gqa_w_qkv, gqa_q_norm, gqa_k_norm, gqa_w_o):
    bsz, n_lat, _ = x.shape
    rows = n_lat // GRID_W
    da_cos, da_sin = axial_rope_tables(rows, DA_HEAD_DIM)
    gqa_cos, gqa_sin = axial_rope_tables(rows, GQA_HEAD_DIM)
    lb_table = jnp.cumsum(jax.nn.softmax(hg_lower_bound.astype(jnp.float32), axis=1), axis=1)
    lb_table = lb_table - lb_table[:, :1]
    silu_c = jax.nn.silu(c)
    silu_cc = jax.nn.silu(c_ctx)
    xl, xc = x, ctx
    for i in range(DEPTH):
        kind, j = i % N_MIXERS, i // N_MIXERS
        need_ctx = i < DEPTH - 1
        mod_l = (silu_c @ w_mod[i] + b_mod[i]).reshape(bsz, N_MOD, 1, D_MODEL)
        mod_c = (silu_cc @ w_mod[i] + b_mod[i]).reshape(N_MOD, D_MODEL)
        g = norm_g[i]
        xl = ffn_sublayer(xl, mod_l[:, 0], mod_l[:, 1], mod_l[:, 2], g[0], g[1],
                          ffn_w_in[i, 0], ffn_w_out[i, 0])
        xc = ffn_sublayer(xc, mod_c[0], mod_c[1], mod_c[2], g[0], g[1],
                          ffn_w_in[i, 0], ffn_w_out[i, 0])
        hl = modulate(xl, g[2], mod_l[:, 3], mod_l[:, 4])
        hc = modulate(xc, g[2], mod_c[3], mod_c[4])
        if kind == 0:
            lam_init = 0.8 - 0.6 * math.exp(-0.3 * i)
            yl, yc = differential_attention(hl, hc, da_w_qkv[j], da_lambda[j], da_subln[j],
                                            da_w_o[j], lam_init, da_cos, da_sin, need_ctx)
        elif kind == 1:
            yl, yc = hgrn2_bidirectional(hl, hc, hg_w_in[j], lb_table[0, i], lb_table[1, i],
                                         hg_norm[j], hg_w_o[j], need_ctx)
        else:
            yl, yc = gqa_attention(hl, hc, gqa_w_qkv[j], gqa_q_norm[j], gqa_k_norm[j],
                                   gqa_w_o[j], gqa_cos, gqa_sin, need_ctx)
        xl = gated_residual(xl, yl, g[3], mod_l[:, 5], 1.0)
        xl = ffn_sublayer(xl, mod_l[:, 6], mod_l[:, 7], mod_l[:, 8], g[4], g[5],
                          ffn_w_in[i, 1], ffn_w_out[i, 1])
        if need_ctx:
            xc = gated_residual(xc, yc, g[3], mod_c[5], 1.0)
            xc = ffn_sublayer(xc, mod_c[6], mod_c[7], mod_c[8], g[4], g[5],
                              ffn_w_in[i, 1], ffn_w_out[i, 1])
    return xl
```

```python
import functools
import math

import numpy as np
import jax
import jax.numpy as jnp
from jax import lax
from jax.experimental import pallas as pl
from jax.experimental.pallas import tpu as pltpu

F32 = jnp.float32
BF16 = jnp.bfloat16

NORM_EPS = 1e-6
ROPE_THETA = 10000.0
GRID_W = 64
N_MOD = 9
N_MIXERS = 3
MACARON_WEIGHT = 0.5
LANES = 128
SUBLANES = 8
HEAD_W = 128
N_HEADS = 8
GQA_GROUP = 4
VMEM_LIMIT = 56 * 1024 * 1024

TOKEN_TILE = 512
DA_Q_TILE = 256
GQA_Q_TILE = 128
HG_CHUNK = 64
HG_DIAG = SUBLANES


def _params(*sem):
    return pltpu.CompilerParams(dimension_semantics=sem, vmem_limit_bytes=VMEM_LIMIT)


def _resident(shape):
    zeros = (0,) * len(shape)
    return pl.BlockSpec(shape, lambda *_: zeros, pipeline_mode=pl.Buffered(1))


def _rms(x, g):
    return x * lax.rsqrt(jnp.mean(x * x, axis=-1, keepdims=True) + NORM_EPS) * g


def _modulated(x, g, shift, scale):
    return _rms(x, g) * (1.0 + scale) + shift


def _silu(x):
    return x * jax.nn.sigmoid(x)


def _mod_kernel(c_ref, w_ref, b_ref, o_ref):
    s = _silu(c_ref[...])
    o_ref[0] = jnp.dot(s, w_ref[0], preferred_element_type=F32,
                       precision=lax.Precision.HIGHEST) + b_ref[0]


def _modulation(c_rows, w_mod, b_mod):
    depth, d, n = w_mod.shape
    rows = c_rows.shape[0]
    tn = n // 8
    return pl.pallas_call(
        _mod_kernel,
        grid=(depth, n // tn),
        in_specs=[pl.BlockSpec((rows, d), lambda i, j: (0, 0)),
                  pl.BlockSpec((1, d, tn), lambda i, j: (i, 0, j)),
                  pl.BlockSpec((1, 1, tn), lambda i, j: (i, 0, j))],
        out_specs=pl.BlockSpec((1, rows, tn), lambda i, j: (i, 0, j)),
        out_shape=jax.ShapeDtypeStruct((depth, rows, n), F32),
        compiler_params=_params("arbitrary", "arbitrary"),
        name="modulation",
    )(c_rows, w_mod, b_mod.reshape(depth, 1, n))


class _Layout:
    def __init__(self, batch, seq, ctx):
        self.batch, self.seq, self.ctx = batch, seq, ctx
        self.n_lat = batch * seq
        self.n_all = self.n_lat + batch * ctx
        tm = TOKEN_TILE
        assert seq % tm == 0 and (batch * ctx) % tm == 0
        self.tm = tm
        self.lat_tiles = self.n_lat // tm
        self.all_tiles = self.n_all // tm
        self.tiles_per_sample = seq // tm

    def mod_spec(self, d):
        lat_tiles, tps, batch = self.lat_tiles, self.tiles_per_sample, self.batch
        return pl.BlockSpec((None, N_MOD, d),
                            lambda t: (jnp.where(t < lat_tiles, t // tps, batch), 0, 0))

    def rope_spec(self):
        lat_tiles, tps = self.lat_tiles, self.tiles_per_sample
        return pl.BlockSpec((self.tm, LANES),
                            lambda t: (jnp.where(t < lat_tiles, t % tps, tps), 0))

    def rows(self, width, col=0):
        return pl.BlockSpec((self.tm, width), lambda t: (t, col))


def _ffn_kernel(x_ref, mod_ref, g_ref, win_ref, wout_ref, o_ref, *, mod_base, g_base, d_ff):
    x = x_ref[...]
    shift = mod_ref[mod_base:mod_base + 1, :]
    scale = mod_ref[mod_base + 1:mod_base + 2, :]
    gate = mod_ref[mod_base + 2:mod_base + 3, :]
    h = _modulated(x, g_ref[g_base:g_base + 1, :], shift, scale).astype(BF16)
    gu = jnp.dot(h, win_ref[...], preferred_element_type=F32)
    a = (_silu(gu[:, :d_ff]) * gu[:, d_ff:]).astype(BF16)
    y = jnp.dot(a, wout_ref[...], preferred_element_type=F32)
    o_ref[...] = x + (MACARON_WEIGHT * gate) * _rms(y, g_ref[g_base + 1:g_base + 2, :])


def _ffn_sublayer(lay, x, mod, g, w_in, w_out, which, n_tiles):
    d = x.shape[1]
    d_ff = w_out.shape[0]
    kern = functools.partial(_ffn_kernel, mod_base=6 * which, g_base=4 * which, d_ff=d_ff)
    return pl.pallas_call(
        kern,
        grid=(n_tiles,),
        in_specs=[lay.rows(d), lay.mod_spec(d), _resident(g.shape),
                  _resident(w_in.shape), _resident(w_out.shape)],
        out_specs=lay.rows(d),
        out_shape=jax.ShapeDtypeStruct((n_tiles * lay.tm, d), F32),
        compiler_params=_params("parallel"),
        name=f"ffn{which}",
    )(x, mod, g, w_in, w_out)


def _mixer_input(x_ref, mod_ref, g_ref):
    return _modulated(x_ref[...], g_ref[2:3, :], mod_ref[3:4, :], mod_ref[4:5, :]).astype(BF16)


def _rope(x, cos, sin):
    return x * cos + pltpu.roll(x, LANES // 2, 1) * sin


def _proj_da_kernel(x_ref, mod_ref, g_ref, w_ref, cos_ref, sin_ref, q_ref, k_ref, v_ref, *, q_scale):
    h = _mixer_input(x_ref, mod_ref, g_ref)
    qkv = jnp.dot(h, w_ref[...], preferred_element_type=F32)
    cos, sin = cos_ref[...], sin_ref[...]
    width = q_ref.shape[1]
    for hb in range(width // HEAD_W):
        lo = hb * HEAD_W
        q_ref[:, lo:lo + HEAD_W] = (_rope(qkv[:, lo:lo + HEAD_W], cos, sin) * q_scale).astype(BF16)
        k_ref[:, lo:lo + HEAD_W] = _rope(qkv[:, width + lo:width + lo + HEAD_W], cos, sin).astype(BF16)
    v_ref[...] = qkv[:, 2 * width:].astype(BF16)


def _proj_gqa_kernel(x_ref, mod_ref, g_ref, w_ref, cos_ref, sin_ref, qn_ref, kn_ref,
                     q_ref, k_ref, v_ref, *, q_scale):
    h = _mixer_input(x_ref, mod_ref, g_ref)
    qkv = jnp.dot(h, w_ref[...], preferred_element_type=F32)
    cos, sin = cos_ref[...], sin_ref[...]
    qw, kw = q_ref.shape[1], k_ref.shape[1]
    for hb in range(qw // HEAD_W):
        lo = hb * HEAD_W
        xq = _rms(qkv[:, lo:lo + HEAD_W], qn_ref[...])
        q_ref[:, lo:lo + HEAD_W] = (_rope(xq, cos, sin) * q_scale).astype(BF16)
    for hb in range(kw // HEAD_W):
        lo = hb * HEAD_W
        xk = _rms(qkv[:, qw + lo:qw + lo + HEAD_W], kn_ref[...])
        k_ref[:, lo:lo + HEAD_W] = _rope(xk, cos, sin).astype(BF16)
    v_ref[...] = qkv[:, qw + kw:].astype(BF16)


def _proj_hg_kernel(x_ref, mod_ref, g_ref, w_ref, lb_ref, o_ref):
    h = _mixer_input(x_ref, mod_ref, g_ref)
    p = jnp.dot(h, w_ref[...], preferred_element_type=F32)
    d = x_ref.shape[1]
    o_ref[:, 0:d] = _silu(p[:, 0:d])
    for k in range(2):
        lb = lb_ref[k:k + 1, :]
        f = lb + (1.0 - lb) * jax.nn.sigmoid(p[:, (1 + k) * d:(2 + k) * d])
        o_ref[:, (1 + k) * d:(2 + k) * d] = jnp.log(f)
    o_ref[:, 3 * d:] = p[:, 3 * d:]


def _project(lay, kern, x, mod, g, w, extra, extra_specs, out_widths, out_dtype, name):
    d = x.shape[1]
    outs = pl.pallas_call(
        kern,
        grid=(lay.all_tiles,),
        in_specs=[lay.rows(d), lay.mod_spec(d), _resident(g.shape), _resident(w.shape)] + extra_specs,
        out_specs=[lay.rows(wd) for wd in out_widths],
        out_shape=[jax.ShapeDtypeStruct((lay.n_all, wd), out_dtype) for wd in out_widths],
        compiler_params=_params("parallel"),
        name=name,
    )(x, mod, g, w, *extra)
    return outs


def _nt_dot(a, b):
    return lax.dot_general(a, b, (((1,), (1,)), ((), ())), preferred_element_type=F32)


def _softmax_parts(q, keys):
    s = [_nt_dot(q, k) for k in keys]
    m = functools.reduce(jnp.maximum, [jnp.max(x, axis=-1, keepdims=True) for x in s])
    e = [jnp.exp(x - m) for x in s]
    l = functools.reduce(jnp.add, [jnp.sum(x, axis=-1, keepdims=True) for x in e])
    return e, 1.0 / l


def _da_attend(q, keys, values, lam, gain):
    tq = q.shape[0]
    lane = lax.broadcasted_iota(jnp.int32, (1, LANES), 1)
    first_map = (lane % (LANES // 2)) < (LANES // 4)
    zero = jnp.zeros_like(q)
    qq = jnp.concatenate([jnp.where(first_map, q, zero), jnp.where(first_map, zero, q)], axis=0)
    e, r = _softmax_parts(qq, keys)
    r0, r1 = r[:tq], r[tq:] * lam
    o = None
    for x, v in zip(e, values):
        a = (x[:tq] * r0 - x[tq:] * r1).astype(BF16)
        pv = jnp.dot(a, v, preferred_element_type=F32)
        o = pv if o is None else o + pv
    return _rms(o, gain)


def _da_attn_kernel(q_ref, kl_ref, kc_ref, vl_ref, vc_ref, lam_ref, g_ref, o_ref, *, nq, lam_init):
    lp = lam_ref[...]
    lam = (jnp.exp(jnp.sum(lp[0:1] * lp[1:2], axis=-1, keepdims=True))
           - jnp.exp(jnp.sum(lp[2:3] * lp[3:4], axis=-1, keepdims=True)) + lam_init)
    gain = g_ref[...] * (1.0 - lam_init)
    i = pl.program_id(2)

    @pl.when(i < nq)
    def _():
        o_ref[...] = _da_attend(q_ref[...], [kl_ref[...], kc_ref[...]],
                                [vl_ref[...], vc_ref[...]], lam, gain).astype(BF16)

    @pl.when(i >= nq)
    def _():
        o_ref[...] = _da_attend(q_ref[...], [kc_ref[...]], [vc_ref[...]], lam, gain).astype(BF16)


def _gqa_attend(q_ref, keys, values, o_ref):
    tq = q_ref.shape[0]
    qs = jnp.concatenate([q_ref[:, g * HEAD_W:(g + 1) * HEAD_W] for g in range(GQA_GROUP)], axis=0)
    e, r = _softmax_parts(qs, keys)
    o = None
    for x, v in zip(e, values):
        pv = jnp.dot((x * r).astype(BF16), v, preferred_element_type=F32)
        o = pv if o is None else o + pv
    for g in range(GQA_GROUP):
        o_ref[:, g * HEAD_W:(g + 1) * HEAD_W] = o[g * tq:(g + 1) * tq].astype(BF16)


def _gqa_attn_kernel(q_ref, kl_ref, kc_ref, vl_ref, vc_ref, o_ref, *, nq):
    i = pl.program_id(2)

    @pl.when(i < nq)
    def _():
        _gqa_attend(q_ref, [kl_ref[...], kc_ref[...]], [vl_ref[...], vc_ref[...]], o_ref)

    @pl.when(i >= nq)
    def _():
        _gqa_attend(q_ref, [kc_ref[...]], [vc_ref[...]], o_ref)


def _attention(lay, kern, q, k, v, extra, extra_specs, *, kv_heads, q_block_w, tq, need_ctx, name):
    batch, seq, ctx, n_lat = lay.batch, lay.seq, lay.ctx, lay.n_lat
    assert seq % tq == 0 and ctx % tq == 0 and n_lat % ctx == 0
    nq, nqc = seq // tq, (ctx // tq if need_ctx else 0)
    lat_rows, ctx_base = n_lat // tq, n_lat // ctx

    def q_map(b, h, i):
        return (jnp.where(i < nq, b * nq + i, lat_rows + b * (ctx // tq) + (i - nq)), h)

    kv_lat = pl.BlockSpec((seq, HEAD_W), lambda b, h, i: (b, h))
    kv_ctx = pl.BlockSpec((ctx, HEAD_W), lambda b, h, i: (ctx_base + b, h))
    n_rows = lay.n_all if need_ctx else n_lat
    return pl.pallas_call(
        functools.partial(kern, nq=nq),
        grid=(batch, kv_heads, nq + nqc),
        in_specs=[pl.BlockSpec((tq, q_block_w), q_map), kv_lat, kv_ctx, kv_lat, kv_ctx] + extra_specs,
        out_specs=pl.BlockSpec((tq, q_block_w), q_map),
        out_shape=jax.ShapeDtypeStruct((n_rows, q.shape[1]), BF16),
        compiler_params=_params("parallel", "parallel", "arbitrary"),
        name=name,
    )(q, k, k, v, v, *extra)


def _hg_scan_kernel(q_ref, lf_ref, v_ref, o_ref, st_ref, *, reverse):
    c = q_ref.shape[0]
    j = pl.program_id(1)

    @pl.when(j == 0)
    def _():
        st_ref[...] = jnp.zeros_like(st_ref)

    row = lax.broadcasted_iota(jnp.int32, (c, c), 0)
    col = lax.broadcasted_iota(jnp.int32, (c, c), 1)
    later, earlier = (col, row) if reverse else (row, col)
    tri = (earlier <= later).astype(F32)
    cum_all = jnp.dot(tri, lf_ref[...], preferred_element_type=F32, precision=lax.Precision.HIGHEST)

    levels = []
    half = c // 2
    while half >= HG_DIAG:
        same = (row // (2 * half)) == (col // (2 * half))
        q_side = (later % (2 * half)) >= half
        k_side = (earlier % (2 * half)) < half
        levels.append((half, same & q_side & k_side))
        half //= 2
    t_loc = lax.broadcasted_iota(jnp.int32, (c // HG_DIAG, HG_DIAG, HEAD_W), 1)

    for h in range(q_ref.shape[1] // HEAD_W):
        sl = slice(h * HEAD_W, (h + 1) * HEAD_W)
        q, lf, v = q_ref[:, sl], lf_ref[:, sl], v_ref[:, sl]
        cum = cum_all[:, sl]
        k = 1.0 - jnp.exp(lf)
        v_b = v.astype(BF16)

        att = jnp.zeros((c, c), F32)
        for half, mask in levels:
            nb = c // (2 * half)
            c3 = cum.reshape(nb, 2 * half, HEAD_W)
            b_row = half if reverse else half - 1
            ref = jnp.broadcast_to(c3[:, b_row:b_row + 1, :], c3.shape).reshape(c, HEAD_W)
            qt = (q * jnp.exp(jnp.minimum(cum - ref, 0.0))).astype(BF16)
            kt = (k * jnp.exp(jnp.minimum(ref - cum, 0.0))).astype(BF16)
            att = jnp.where(mask, _nt_dot(qt, kt), att)
        o = jnp.dot(att.astype(BF16), v_b, preferred_element_type=F32)

        nb = c // HG_DIAG
        q3, c3 = q.reshape(nb, HG_DIAG, HEAD_W), cum.reshape(nb, HG_DIAG, HEAD_W)
        k3, v3 = k.reshape(nb, HG_DIAG, HEAD_W), v.reshape(nb, HG_DIAG, HEAD_W)
        od = jnp.zeros((nb, HG_DIAG, HEAD_W), F32)
        for s in range(HG_DIAG):
            z = q3 * jnp.exp(jnp.minimum(c3 - c3[:, s:s + 1, :], 0.0)) * k3[:, s:s + 1, :]
            w = jnp.sum(z, axis=-1, keepdims=True)
            valid = (t_loc <= s) if reverse else (t_loc >= s)
            od = od + jnp.where(valid, w * v3[:, s:s + 1, :], 0.0)
        o = o + od.reshape(c, HEAD_W)

        st = st_ref[h]
        end = 0 if reverse else c - 1
        total = cum[end:end + 1, :]
        o = o + _nt_dot((q * jnp.exp(cum)).astype(BF16), st.astype(BF16))
        kd = (k * jnp.exp(total - cum)).astype(BF16)
        upd = lax.dot_general(v_b, kd, (((0,), (0,)), ((), ())), preferred_element_type=F32)
        st_ref[h] = st * jnp.exp(total) + upd
        o_ref[:, sl] = o


def _hg_scan(lay, p, d, reverse):
    batch, seq, ctx, n_lat = lay.batch, lay.seq, lay.ctx, lay.n_lat
    c = HG_CHUNK
    assert seq % c == 0 and ctx % c == 0
    nlc, ncc = seq // c, ctx // c
    ctx_base = n_lat // c

    def blk(b, j):
        c_ctx = (ncc - 1 - j) if reverse else j
        c_lat = (nlc - 1 - (j - ncc)) if reverse else (j - ncc)
        return jnp.where(j < ncc, ctx_base + b * ncc + c_ctx, b * nlc + c_lat)

    def col_spec(colblk):
        return pl.BlockSpec((c, d), lambda b, j: (blk(b, j), colblk))

    return pl.pallas_call(
        functools.partial(_hg_scan_kernel, reverse=reverse),
        grid=(batch, ncc + nlc),
        in_specs=[col_spec(0), col_spec(2 if reverse else 1), col_spec(3)],
        out_specs=col_spec(0),
        out_shape=jax.ShapeDtypeStruct((lay.n_all, d), F32),
        scratch_shapes=[pltpu.VMEM((d // HEAD_W, HEAD_W, HEAD_W), F32)],
        compiler_params=_params("arbitrary", "arbitrary"),
        name="hg_scan_bwd" if reverse else "hg_scan_fwd",
    )(p, p, p)


def _out_kernel(x_ref, y_ref, mod_ref, g_ref, w_ref, o_ref):
    y = jnp.dot(y_ref[...], w_ref[...], preferred_element_type=F32)
    o_ref[...] = x_ref[...] + mod_ref[5:6, :] * _rms(y, g_ref[3:4, :])


def _out_hg_kernel(x_ref, of_ref, ob_ref, gate_ref, hn_ref, mod_ref, g_ref, w_ref, o_ref):
    o = of_ref[...] + ob_ref[...]
    gate = gate_ref[...]
    parts = []
    for h in range(o.shape[1] // HEAD_W):
        sl = slice(h * HEAD_W, (h + 1) * HEAD_W)
        parts.append((_rms(o[:, sl], hn_ref[...]) * _silu(gate[:, sl])).astype(BF16))
    y = jnp.dot(jnp.concatenate(parts, axis=1), w_ref[...], preferred_element_type=F32)
    o_ref[...] = x_ref[...] + mod_ref[5:6, :] * _rms(y, g_ref[3:4, :])


def _mixer_output(lay, kern, x, ys, y_specs, mod, g, w, extra, n_tiles, name):
    d = x.shape[1]
    return pl.pallas_call(
        kern,
        grid=(n_tiles,),
        in_specs=[lay.rows(d)] + y_specs + [_resident(e.shape) for e in extra]
                 + [lay.mod_spec(d), _resident(g.shape), _resident(w.shape)],
        out_specs=lay.rows(d),
        out_shape=jax.ShapeDtypeStruct((n_tiles * lay.tm, d), F32),
        compiler_params=_params("parallel"),
        name=name,
    )(x, *ys, *extra, mod, g, w)


def _rope_tables(rows, head_dim, n_id_rows, maps):
    pairs = head_dim // 4
    inv_freq = jnp.power(ROPE_THETA, -jnp.arange(pairs, dtype=F32) / pairs)
    r = jnp.repeat(jnp.arange(rows, dtype=F32), GRID_W)
    col = jnp.tile(jnp.arange(GRID_W, dtype=F32), rows)
    ang = jnp.concatenate([r[:, None] * inv_freq, col[:, None] * inv_freq], axis=-1)
    cos, sin = jnp.cos(ang), jnp.sin(ang)
    cos = jnp.tile(cos, (1, 2 * maps))
    sin = jnp.concatenate([-jnp.tile(sin, (1, maps)), jnp.tile(sin, (1, maps))], axis=-1)
    cos = jnp.concatenate([cos, jnp.ones((n_id_rows, LANES), F32)], axis=0)
    sin = jnp.concatenate([sin, jnp.zeros((n_id_rows, LANES), F32)], axis=0)
    return cos, sin


def _da_head_perm():
    p = np.arange(2)[:, None, None]
    m = np.arange(2)[None, :, None]
    j = np.arange(HEAD_W // 4)[None, None, :]
    return (m * (HEAD_W // 2) + 2 * j + p).reshape(-1)


def _gqa_head_perm():
    p = np.arange(2)[:, None]
    j = np.arange(HEAD_W // 2)[None, :]
    return (2 * j + p).reshape(-1)


def _permute_heads(n_blocks, perm, n_tail):
    idx = (np.arange(n_blocks)[:, None] * HEAD_W + perm[None, :]).reshape(-1)
    return np.concatenate([idx, n_blocks * HEAD_W + np.arange(n_tail)])


def kernel(x, c, ctx, c_ctx, w_mod, b_mod, norm_g, ffn_w_in, ffn_w_out, da_w_qkv, da_lambda, da_subln, da_w_o, hg_w_in, hg_lower_bound, hg_norm, hg_w_o, gqa_w_qkv, gqa_q_norm, gqa_k_norm, gqa_w_o):
    batch, seq, d = x.shape
    n_ctx = ctx.shape[1]
    depth = w_mod.shape[0]
    assert d == N_HEADS * HEAD_W
    lay = _Layout(batch, seq, n_ctx)
    rows = seq // GRID_W

    pad = (-(batch + 1)) % SUBLANES
    c_rows = jnp.concatenate([c, c_ctx[None, :], jnp.zeros((pad, d), F32)], axis=0)
    mod_all = _modulation(c_rows, w_mod, b_mod)[:, :batch + 1].reshape(depth, batch + 1, N_MOD, d)

    da_cos, da_sin = _rope_tables(rows, HEAD_W // 2, lay.tm, 2)
    gqa_cos, gqa_sin = _rope_tables(rows, HEAD_W, lay.tm, 1)
    da_cols = _permute_heads(2 * N_HEADS, _da_head_perm(), d)
    gqa_perm = _gqa_head_perm()
    gqa_cols = _permute_heads(N_HEADS + N_HEADS // GQA_GROUP, gqa_perm, d // GQA_GROUP)
    lb_table = jnp.cumsum(jax.nn.softmax(hg_lower_bound.astype(F32), axis=1), axis=1)
    lb_table = lb_table - lb_table[:, :1]

    xa = jnp.concatenate([x.reshape(batch * seq, d), ctx.reshape(batch * n_ctx, d)], axis=0)
    for i in range(depth):
        kind, j = i % N_MIXERS, i // N_MIXERS
        need_ctx = i < depth - 1
        mod, g = mod_all[i], norm_g[i]
        w_in, w_out = ffn_w_in[i].astype(BF16), ffn_w_out[i].astype(BF16)
        xa = _ffn_sublayer(lay, xa, mod, g, w_in[0], w_out[0], 0, lay.all_tiles)
        out_tiles = lay.all_tiles if need_ctx else lay.lat_tiles

        if kind == 0:
            lam_init = 0.8 - 0.6 * math.exp(-0.3 * i)
            w = da_w_qkv[j][:, da_cols].astype(BF16)
            q, k, v = _project(
                lay, functools.partial(_proj_da_kernel, q_scale=(HEAD_W // 2) ** -0.5),
                xa, mod, g, w, [da_cos, da_sin], [lay.rope_spec(), lay.rope_spec()],
                [d, d, d], BF16, "proj_da")
            y = _attention(
                lay, functools.partial(_da_attn_kernel, lam_init=lam_init), q, k, v,
                [da_lambda[j], da_subln[j][None, :]],
                [_resident(da_lambda[j].shape), _resident((1, HEAD_W))],
                kv_heads=N_HEADS, q_block_w=HEAD_W, tq=DA_Q_TILE, need_ctx=need_ctx, name="attn_da")
            xa = _mixer_output(lay, _out_kernel, xa, [y], [lay.rows(d)], mod, g,
                               da_w_o[j].astype(BF16), [], out_tiles, "out_da")
        elif kind == 1:
            p, = _project(lay, _proj_hg_kernel, xa, mod, g, hg_w_in[j].astype(BF16),
                          [lb_table[:, i]], [_resident((2, d))], [5 * d], F32, "proj_hg")
            o_f = _hg_scan(lay, p, d, reverse=False)
            o_b = _hg_scan(lay, p, d, reverse=True)
            xa = _mixer_output(lay, _out_hg_kernel, xa, [o_f, o_b, p],
                               [lay.rows(d), lay.rows(d), lay.rows(d, 4)], mod, g,
                               hg_w_o[j].astype(BF16), [hg_norm[j][None, :]], out_tiles, "out_hg")
        else:
            kvw = d // GQA_GROUP
            w = gqa_w_qkv[j][:, gqa_cols].astype(BF16)
            q, k, v = _project(
                lay, functools.partial(_proj_gqa_kernel, q_scale=HEAD_W ** -0.5),
                xa, mod, g, w,
                [gqa_cos, gqa_sin, gqa_q_norm[j][gqa_perm][None, :], gqa_k_norm[j][gqa_perm][None, :]],
                [lay.rope_spec(), lay.rope_spec(), _resident((1, HEAD_W)), _resident((1, HEAD_W))],
                [d, kvw, kvw], BF16, "proj_gqa")
            y = _attention(lay, _gqa_attn_kernel, q, k, v, [], [],
                           kv_heads=N_HEADS // GQA_GROUP, q_block_w=GQA_GROUP * HEAD_W,
                           tq=GQA_Q_TILE, need_ctx=need_ctx, name="attn_gqa")
            xa = _mixer_output(lay, _out_kernel, xa, [y], [lay.rows(d)], mod, g,
                               gqa_w_o[j].astype(BF16), [], out_tiles, "out_gqa")

        xa = _ffn_sublayer(lay, xa, mod, g, w_in[1], w_out[1], 1, out_tiles)
    return xa.reshape(batch, seq, d)
```

```python
import functools
import math

import numpy as np
import jax
import jax.numpy as jnp
from jax import lax
from jax.experimental import pallas as pl
from jax.experimental.pallas import tpu as pltpu

F32 = jnp.float32
BF16 = jnp.bfloat16

NORM_EPS = 1e-6
ROPE_THETA = 10000.0
GRID_W = 64
N_MOD = 9
N_MIXERS = 3
MACARON_WEIGHT = 0.5
LOG2_E = math.log2(math.e)
LANES = 128
SUBLANES = 8
HEAD_W = 128
N_HEADS = 8
GQA_GROUP = 4
VMEM_LIMIT = 56 * 1024 * 1024

TOKEN_TILE = 512
DA_Q_TILE = 256
GQA_Q_TILE = 128
KEY_BLOCK = 512
HG_CHUNK = 64
HG_DIAG = SUBLANES


def _params(*sem):
    return pltpu.CompilerParams(dimension_semantics=sem, vmem_limit_bytes=VMEM_LIMIT)


def _resident(shape):
    zeros = (0,) * len(shape)
    return pl.BlockSpec(shape, lambda *_: zeros, pipeline_mode=pl.Buffered(1))


def _rms(x, g):
    return x * lax.rsqrt(jnp.mean(x * x, axis=-1, keepdims=True) + NORM_EPS) * g


def _modulated(x, g, shift, scale):
    return _rms(x, g) * (1.0 + scale) + shift


def _silu(x):
    return x * jax.nn.sigmoid(x)


def _mod_kernel(c_ref, w_ref, b_ref, o_ref):
    s = _silu(c_ref[...])
    o_ref[0] = jnp.dot(s, w_ref[0], preferred_element_type=F32,
                       precision=lax.Precision.HIGHEST) + b_ref[0]


def _modulation(c_rows, w_mod, b_mod):
    depth, d, n = w_mod.shape
    rows = c_rows.shape[0]
    tn = n // 8
    return pl.pallas_call(
        _mod_kernel,
        grid=(depth, n // tn),
        in_specs=[pl.BlockSpec((rows, d), lambda i, j: (0, 0)),
                  pl.BlockSpec((1, d, tn), lambda i, j: (i, 0, j)),
                  pl.BlockSpec((1, 1, tn), lambda i, j: (i, 0, j))],
        out_specs=pl.BlockSpec((1, rows, tn), lambda i, j: (i, 0, j)),
        out_shape=jax.ShapeDtypeStruct((depth, rows, n), F32),
        compiler_params=_params("arbitrary", "arbitrary"),
        name="modulation",
    )(c_rows, w_mod, b_mod.reshape(depth, 1, n))


class _Layout:
    def __init__(self, batch, seq, ctx):
        self.batch, self.seq, self.ctx = batch, seq, ctx
        self.n_lat = batch * seq
        self.n_all = self.n_lat + batch * ctx
        tm = TOKEN_TILE
        assert seq % tm == 0 and (batch * ctx) % tm == 0
        self.tm = tm
        self.lat_tiles = self.n_lat // tm
        self.all_tiles = self.n_all // tm
        self.tiles_per_sample = seq // tm

    def mod_spec(self, d):
        lat_tiles, tps, batch = self.lat_tiles, self.tiles_per_sample, self.batch
        return pl.BlockSpec((None, N_MOD, d),
                            lambda t: (jnp.where(t < lat_tiles, t // tps, batch), 0, 0))

    def rope_spec(self):
        lat_tiles, tps = self.lat_tiles, self.tiles_per_sample
        return pl.BlockSpec((self.tm, LANES),
                            lambda t: (jnp.where(t < lat_tiles, t % tps, tps), 0))

    def rows(self, width, col=0):
        return pl.BlockSpec((self.tm, width), lambda t: (t, col))


def _ffn_kernel(x_ref, mod_ref, g_ref, win_ref, wout_ref, o_ref, *, mod_base, g_base, d_ff):
    x = x_ref[...]
    shift = mod_ref[mod_base:mod_base + 1, :]
    scale = mod_ref[mod_base + 1:mod_base + 2, :]
    gate = mod_ref[mod_base + 2:mod_base + 3, :]
    h = _modulated(x, g_ref[g_base:g_base + 1, :], shift, scale).astype(BF16)
    gu = jnp.dot(h, win_ref[...], preferred_element_type=F32)
    a = (_silu(gu[:, :d_ff]) * gu[:, d_ff:]).astype(BF16)
    y = jnp.dot(a, wout_ref[...], preferred_element_type=F32)
    o_ref[...] = x + (MACARON_WEIGHT * gate) * _rms(y, g_ref[g_base + 1:g_base + 2, :])


def _ffn_sublayer(lay, x, mod, g, w_in, w_out, which, n_tiles):
    d = x.shape[1]
    d_ff = w_out.shape[0]
    kern = functools.partial(_ffn_kernel, mod_base=6 * which, g_base=4 * which, d_ff=d_ff)
    return pl.pallas_call(
        kern,
        grid=(n_tiles,),
        in_specs=[lay.rows(d), lay.mod_spec(d), _resident(g.shape),
                  _resident(w_in.shape), _resident(w_out.shape)],
        out_specs=lay.rows(d),
        out_shape=jax.ShapeDtypeStruct((n_tiles * lay.tm, d), F32),
        compiler_params=_params("parallel"),
        name=f"ffn{which}",
    )(x, mod, g, w_in, w_out)


def _mixer_input(x_ref, mod_ref, g_ref):
    return _modulated(x_ref[...], g_ref[2:3, :], mod_ref[3:4, :], mod_ref[4:5, :]).astype(BF16)


def _rope(x, cos, sin):
    return x * cos + pltpu.roll(x, LANES // 2, 1) * sin


def _proj_da_kernel(x_ref, mod_ref, g_ref, w_ref, cos_ref, sin_ref, q_ref, k_ref, v_ref, *, q_scale):
    h = _mixer_input(x_ref, mod_ref, g_ref)
    qkv = jnp.dot(h, w_ref[...], preferred_element_type=F32)
    cos, sin = cos_ref[...], sin_ref[...]
    width = q_ref.shape[1]
    for hb in range(width // HEAD_W):
        lo = hb * HEAD_W
        q_ref[:, lo:lo + HEAD_W] = (_rope(qkv[:, lo:lo + HEAD_W], cos, sin) * q_scale).astype(BF16)
        k_ref[:, lo:lo + HEAD_W] = _rope(qkv[:, width + lo:width + lo + HEAD_W], cos, sin).astype(BF16)
    v_ref[...] = qkv[:, 2 * width:].astype(BF16)


def _proj_gqa_kernel(x_ref, mod_ref, g_ref, w_ref, cos_ref, sin_ref, qn_ref, kn_ref,
                     q_ref, k_ref, v_ref, *, q_scale):
    h = _mixer_input(x_ref, mod_ref, g_ref)
    qkv = jnp.dot(h, w_ref[...], preferred_element_type=F32)
    cos, sin = cos_ref[...], sin_ref[...]
    qw, kw = q_ref.shape[1], k_ref.shape[1]
    for hb in range(qw // HEAD_W):
        lo = hb * HEAD_W
        xq = _rms(qkv[:, lo:lo + HEAD_W], qn_ref[...])
        q_ref[:, lo:lo + HEAD_W] = (_rope(xq, cos, sin) * q_scale).astype(BF16)
    for hb in range(kw // HEAD_W):
        lo = hb * HEAD_W
        xk = _rms(qkv[:, qw + lo:qw + lo + HEAD_W], kn_ref[...])
        k_ref[:, lo:lo + HEAD_W] = _rope(xk, cos, sin).astype(BF16)
    v_ref[...] = qkv[:, qw + kw:].astype(BF16)


def _proj_hg_kernel(x_ref, mod_ref, g_ref, w_ref, lb_ref, o_ref):
    h = _mixer_input(x_ref, mod_ref, g_ref)
    p = jnp.dot(h, w_ref[...], preferred_element_type=F32)
    d = x_ref.shape[1]
    o_ref[:, 0:d] = _silu(p[:, 0:d])
    for k in range(2):
        lb = lb_ref[k:k + 1, :]
        f = lb + (1.0 - lb) * jax.nn.sigmoid(p[:, (1 + k) * d:(2 + k) * d])
        o_ref[:, (1 + k) * d:(2 + k) * d] = jnp.log2(f)
    o_ref[:, 3 * d:] = p[:, 3 * d:]


def _project(lay, kern, x, mod, g, w, extra, extra_specs, out_widths, out_dtype, name):
    d = x.shape[1]
    outs = pl.pallas_call(
        kern,
        grid=(lay.all_tiles,),
        in_specs=[lay.rows(d), lay.mod_spec(d), _resident(g.shape), _resident(w.shape)] + extra_specs,
        out_specs=[lay.rows(wd) for wd in out_widths],
        out_shape=[jax.ShapeDtypeStruct((lay.n_all, wd), out_dtype) for wd in out_widths],
        compiler_params=_params("parallel"),
        name=name,
    )(x, mod, g, w, *extra)
    return outs


def _nt_dot(a, b):
    return lax.dot_general(a, b, (((1,), (1,)), ((), ())), preferred_element_type=F32)


def _stage_keys(kl_ref, kc_ref, vl_ref, vc_ref, k_s, v_s):
    seq = kl_ref.shape[0]
    k_s[0:seq, :] = kl_ref[...]
    k_s[seq:, :] = kc_ref[...]
    v_s[0:seq, 0:HEAD_W] = vl_ref[...]
    v_s[seq:, 0:HEAD_W] = vc_ref[...]
    v_s[:, HEAD_W:] = jnp.ones((v_s.shape[0], HEAD_W), BF16)


def _key_blocks(start, stop):
    return [(s, min(KEY_BLOCK, stop - s)) for s in range(start, stop, KEY_BLOCK)]


def _softmax_pv(q, k_s, v_s, blocks):
    m = acc = None
    for start, size in blocks:
        s = _nt_dot(q, k_s[start:start + size, :])
        m_new = jnp.max(s, axis=-1, keepdims=True)
        if m is not None:
            m_new = jnp.maximum(m, m_new)
        e = jnp.exp2(s - m_new).astype(BF16)
        pv = jnp.dot(e, v_s[start:start + size, :], preferred_element_type=F32)
        acc = pv if acc is None else acc * jnp.exp2(m - m_new) + pv
        m = m_new
    return acc[:, :HEAD_W] / acc[:, HEAD_W:]


def _da_attend(q, k_s, v_s, blocks, lam, gain):
    tq = q.shape[0]
    lane = lax.broadcasted_iota(jnp.int32, (1, LANES), 1)
    first_map = (lane % (LANES // 2)) < (LANES // 4)
    zero = jnp.zeros_like(q)
    qq = jnp.concatenate([jnp.where(first_map, q, zero), jnp.where(first_map, zero, q)], axis=0)
    p = _softmax_pv(qq, k_s, v_s, blocks)
    return _rms(p[:tq] - lam * p[tq:], gain)


def _da_attn_kernel(q_ref, kl_ref, kc_ref, vl_ref, vc_ref, lam_ref, g_ref, o_ref, k_s, v_s,
                    *, nq, lam_init):
    lp = lam_ref[...]
    lam = (jnp.exp(jnp.sum(lp[0:1] * lp[1:2], axis=-1, keepdims=True))
           - jnp.exp(jnp.sum(lp[2:3] * lp[3:4], axis=-1, keepdims=True)) + lam_init)
    gain = g_ref[...] * (1.0 - lam_init)
    i = pl.program_id(2)
    seq = kl_ref.shape[0]

    @pl.when(i == 0)
    def _():
        _stage_keys(kl_ref, kc_ref, vl_ref, vc_ref, k_s, v_s)

    total = k_s.shape[0]

    @pl.when(i < nq)
    def _():
        o_ref[...] = _da_attend(q_ref[...], k_s, v_s, _key_blocks(0, total), lam, gain).astype(BF16)

    @pl.when(i >= nq)
    def _():
        o_ref[...] = _da_attend(q_ref[...], k_s, v_s, _key_blocks(seq, total), lam, gain).astype(BF16)


def _gqa_attend(q_ref, k_s, v_s, blocks, o_ref):
    tq = q_ref.shape[0]
    qs = jnp.concatenate([q_ref[:, g * HEAD_W:(g + 1) * HEAD_W] for g in range(GQA_GROUP)], axis=0)
    o = _softmax_pv(qs, k_s, v_s, blocks)
    for g in range(GQA_GROUP):
        o_ref[:, g * HEAD_W:(g + 1) * HEAD_W] = o[g * tq:(g + 1) * tq].astype(BF16)


def _gqa_attn_kernel(q_ref, kl_ref, kc_ref, vl_ref, vc_ref, o_ref, k_s, v_s, *, nq):
    i = pl.program_id(2)
    seq = kl_ref.shape[0]

    @pl.when(i == 0)
    def _():
        _stage_keys(kl_ref, kc_ref, vl_ref, vc_ref, k_s, v_s)

    total = k_s.shape[0]

    @pl.when(i < nq)
    def _():
        _gqa_attend(q_ref, k_s, v_s, _key_blocks(0, total), o_ref)

    @pl.when(i >= nq)
    def _():
        _gqa_attend(q_ref, k_s, v_s, _key_blocks(seq, total), o_ref)


def _attention(lay, kern, q, k, v, extra, extra_specs, *, kv_heads, q_block_w, tq, need_ctx, name):
    batch, seq, ctx, n_lat = lay.batch, lay.seq, lay.ctx, lay.n_lat
    assert seq % tq == 0 and ctx % tq == 0 and n_lat % ctx == 0
    nq, nqc = seq // tq, (ctx // tq if need_ctx else 0)
    lat_rows, ctx_base = n_lat // tq, n_lat // ctx

    def q_map(b, h, i):
        return (jnp.where(i < nq, b * nq + i, lat_rows + b * (ctx // tq) + (i - nq)), h)

    kv_lat = pl.BlockSpec((seq, HEAD_W), lambda b, h, i: (b, h))
    kv_ctx = pl.BlockSpec((ctx, HEAD_W), lambda b, h, i: (ctx_base + b, h))
    n_rows = lay.n_all if need_ctx else n_lat
    return pl.pallas_call(
        functools.partial(kern, nq=nq),
        grid=(batch, kv_heads, nq + nqc),
        in_specs=[pl.BlockSpec((tq, q_block_w), q_map), kv_lat, kv_ctx, kv_lat, kv_ctx] + extra_specs,
        out_specs=pl.BlockSpec((tq, q_block_w), q_map),
        out_shape=jax.ShapeDtypeStruct((n_rows, q.shape[1]), BF16),
        scratch_shapes=[pltpu.VMEM((seq + ctx, HEAD_W), BF16),
                        pltpu.VMEM((seq + ctx, 2 * HEAD_W), BF16)],
        compiler_params=_params("parallel", "parallel", "arbitrary"),
        name=name,
    )(q, k, k, v, v, *extra)


def _hg_scan_kernel(q_ref, lf_ref, v_ref, o_ref, st_ref, *, reverse):
    c = q_ref.shape[0]
    j = pl.program_id(1)

    @pl.when(j == 0)
    def _():
        st_ref[...] = jnp.zeros_like(st_ref)

    row = lax.broadcasted_iota(jnp.int32, (c, c), 0)
    col = lax.broadcasted_iota(jnp.int32, (c, c), 1)
    later, earlier = (col, row) if reverse else (row, col)
    tri = (earlier <= later).astype(F32)
    cum_all = jnp.dot(tri, lf_ref[...], preferred_element_type=F32, precision=lax.Precision.HIGHEST)

    levels = []
    half = c // 2
    while half >= HG_DIAG:
        same = (row // (2 * half)) == (col // (2 * half))
        q_side = (later % (2 * half)) >= half
        k_side = (earlier % (2 * half)) < half
        levels.append((half, same & q_side & k_side))
        half //= 2
    t_loc = lax.broadcasted_iota(jnp.int32, (c // HG_DIAG, HG_DIAG, HEAD_W), 1)

    for h in range(q_ref.shape[1] // HEAD_W):
        sl = slice(h * HEAD_W, (h + 1) * HEAD_W)
        q, lf, v = q_ref[:, sl], lf_ref[:, sl], v_ref[:, sl]
        cum = cum_all[:, sl]
        k = 1.0 - jnp.exp2(lf)
        v_b = v.astype(BF16)

        att = jnp.zeros((c, c), F32)
        for half, mask in levels:
            nb = c // (2 * half)
            c3 = cum.reshape(nb, 2 * half, HEAD_W)
            b_row = half if reverse else half - 1
            ref = jnp.broadcast_to(c3[:, b_row:b_row + 1, :], c3.shape).reshape(c, HEAD_W)
            qt = (q * jnp.exp2(jnp.minimum(cum - ref, 0.0))).astype(BF16)
            kt = (k * jnp.exp2(jnp.minimum(ref - cum, 0.0))).astype(BF16)
            att = jnp.where(mask, _nt_dot(qt, kt), att)
        o = jnp.dot(att.astype(BF16), v_b, preferred_element_type=F32)

        nb = c // HG_DIAG
        q3, c3 = q.reshape(nb, HG_DIAG, HEAD_W), cum.reshape(nb, HG_DIAG, HEAD_W)
        k3, v3 = k.reshape(nb, HG_DIAG, HEAD_W), v.reshape(nb, HG_DIAG, HEAD_W)
        od = jnp.zeros((nb, HG_DIAG, HEAD_W), F32)
        for s in range(HG_DIAG):
            z = q3 * jnp.exp2(jnp.minimum(c3 - c3[:, s:s + 1, :], 0.0)) * k3[:, s:s + 1, :]
            w = jnp.sum(z, axis=-1, keepdims=True)
            valid = (t_loc <= s) if reverse else (t_loc >= s)
            od = od + jnp.where(valid, w * v3[:, s:s + 1, :], 0.0)
        o = o + od.reshape(c, HEAD_W)

        st = st_ref[h]
        end = 0 if reverse else c - 1
        total = cum[end:end + 1, :]
        o = o + _nt_dot((q * jnp.exp2(cum)).astype(BF16), st.astype(BF16))
        kd = (k * jnp.exp2(total - cum)).astype(BF16)
        upd = lax.dot_general(v_b, kd, (((0,), (0,)), ((), ())), preferred_element_type=F32)
        st_ref[h] = st * jnp.exp2(total) + upd
        o_ref[:, sl] = o


def _hg_scan(lay, p, d, reverse):
    batch, seq, ctx, n_lat = lay.batch, lay.seq, lay.ctx, lay.n_lat
    c = HG_CHUNK
    assert seq % c == 0 and ctx % c == 0
    nlc, ncc = seq // c, ctx // c
    ctx_base = n_lat // c

    def blk(b, j):
        c_ctx = (ncc - 1 - j) if reverse else j
        c_lat = (nlc - 1 - (j - ncc)) if reverse else (j - ncc)
        return jnp.where(j < ncc, ctx_base + b * ncc + c_ctx, b * nlc + c_lat)

    def col_spec(colblk):
        return pl.BlockSpec((c, d), lambda b, j: (blk(b, j), colblk))

    return pl.pallas_call(
        functools.partial(_hg_scan_kernel, reverse=reverse),
        grid=(batch, ncc + nlc),
        in_specs=[col_spec(0), col_spec(2 if reverse else 1), col_spec(3)],
        out_specs=col_spec(0),
        out_shape=jax.ShapeDtypeStruct((lay.n_all, d), F32),
        scratch_shapes=[pltpu.VMEM((d // HEAD_W, HEAD_W, HEAD_W), F32)],
        compiler_params=_params("arbitrary", "arbitrary"),
        name="hg_scan_bwd" if reverse else "hg_scan_fwd",
    )(p, p, p)


def _out_kernel(x_ref, y_ref, mod_ref, g_ref, w_ref, o_ref):
    y = jnp.dot(y_ref[...], w_ref[...], preferred_element_type=F32)
    o_ref[...] = x_ref[...] + mod_ref[5:6, :] * _rms(y, g_ref[3:4, :])


def _out_hg_kernel(x_ref, of_ref, ob_ref, gate_ref, hn_ref, mod_ref, g_ref, w_ref, o_ref):
    o = of_ref[...] + ob_ref[...]
    gate = gate_ref[...]
    parts = []
    for h in range(o.shape[1] // HEAD_W):
        sl = slice(h * HEAD_W, (h + 1) * HEAD_W)
        parts.append((_rms(o[:, sl], hn_ref[...]) * _silu(gate[:, sl])).astype(BF16))
    y = jnp.dot(jnp.concatenate(parts, axis=1), w_ref[...], preferred_element_type=F32)
    o_ref[...] = x_ref[...] + mod_ref[5:6, :] * _rms(y, g_ref[3:4, :])


def _mixer_output(lay, kern, x, ys, y_specs, mod, g, w, extra, n_tiles, name):
    d = x.shape[1]
    return pl.pallas_call(
        kern,
        grid=(n_tiles,),
        in_specs=[lay.rows(d)] + y_specs + [_resident(e.shape) for e in extra]
                 + [lay.mod_spec(d), _resident(g.shape), _resident(w.shape)],
        out_specs=lay.rows(d),
        out_shape=jax.ShapeDtypeStruct((n_tiles * lay.tm, d), F32),
        compiler_params=_params("parallel"),
        name=name,
    )(x, *ys, *extra, mod, g, w)


def _rope_tables(rows, head_dim, n_id_rows, maps):
    pairs = head_dim // 4
    inv_freq = jnp.power(ROPE_THETA, -jnp.arange(pairs, dtype=F32) / pairs)
    r = jnp.repeat(jnp.arange(rows, dtype=F32), GRID_W)
    col = jnp.tile(jnp.arange(GRID_W, dtype=F32), rows)
    ang = jnp.concatenate([r[:, None] * inv_freq, col[:, None] * inv_freq], axis=-1)
    cos, sin = jnp.cos(ang), jnp.sin(ang)
    cos = jnp.tile(cos, (1, 2 * maps))
    sin = jnp.concatenate([-jnp.tile(sin, (1, maps)), jnp.tile(sin, (1, maps))], axis=-1)
    cos = jnp.concatenate([cos, jnp.ones((n_id_rows, LANES), F32)], axis=0)
    sin = jnp.concatenate([sin, jnp.zeros((n_id_rows, LANES), F32)], axis=0)
    return cos, sin


def _da_head_perm():
    p = np.arange(2)[:, None, None]
    m = np.arange(2)[None, :, None]
    j = np.arange(HEAD_W // 4)[None, None, :]
    return (m * (HEAD_W // 2) + 2 * j + p).reshape(-1)


def _gqa_head_perm():
    p = np.arange(2)[:, None]
    j = np.arange(HEAD_W // 2)[None, :]
    return (2 * j + p).reshape(-1)


def _permute_heads(n_blocks, perm, n_tail):
    idx = (np.arange(n_blocks)[:, None] * HEAD_W + perm[None, :]).reshape(-1)
    return np.concatenate([idx, n_blocks * HEAD_W + np.arange(n_tail)])


def kernel(x, c, ctx, c_ctx, w_mod, b_mod, norm_g, ffn_w_in, ffn_w_out, da_w_qkv, da_lambda, da_subln, da_w_o, hg_w_in, hg_lower_bound, hg_norm, hg_w_o, gqa_w_qkv, gqa_q_norm, gqa_k_norm, gqa_w_o):
    batch, seq, d = x.shape
    n_ctx = ctx.shape[1]
    depth = w_mod.shape[0]
    assert d == N_HEADS * HEAD_W
    lay = _Layout(batch, seq, n_ctx)
    rows = seq // GRID_W

    pad = (-(batch + 1)) % SUBLANES
    c_rows = jnp.concatenate([c, c_ctx[None, :], jnp.zeros((pad, d), F32)], axis=0)
    mod_all = _modulation(c_rows, w_mod, b_mod)[:, :batch + 1].reshape(depth, batch + 1, N_MOD, d)

    da_cos, da_sin = _rope_tables(rows, HEAD_W // 2, lay.tm, 2)
    gqa_cos, gqa_sin = _rope_tables(rows, HEAD_W, lay.tm, 1)
    da_cols = _permute_heads(2 * N_HEADS, _da_head_perm(), d)
    gqa_perm = _gqa_head_perm()
    gqa_cols = _permute_heads(N_HEADS + N_HEADS // GQA_GROUP, gqa_perm, d // GQA_GROUP)
    lb_table = jnp.cumsum(jax.nn.softmax(hg_lower_bound.astype(F32), axis=1), axis=1)
    lb_table = lb_table - lb_table[:, :1]

    xa = jnp.concatenate([x.reshape(batch * seq, d), ctx.reshape(batch * n_ctx, d)], axis=0)
    for i in range(depth):
        kind, j = i % N_MIXERS, i // N_MIXERS
        need_ctx = i < depth - 1
        mod, g = mod_all[i], norm_g[i]
        w_in, w_out = ffn_w_in[i].astype(BF16), ffn_w_out[i].astype(BF16)
        xa = _ffn_sublayer(lay, xa, mod, g, w_in[0], w_out[0], 0, lay.all_tiles)
        out_tiles = lay.all_tiles if need_ctx else lay.lat_tiles

        if kind == 0:
            lam_init = 0.8 - 0.6 * math.exp(-0.3 * i)
            w = da_w_qkv[j][:, da_cols].astype(BF16)
            q, k, v = _project(
                lay, functools.partial(_proj_da_kernel, q_scale=LOG2_E * (HEAD_W // 2) ** -0.5),
                xa, mod, g, w, [da_cos, da_sin], [lay.rope_spec(), lay.rope_spec()],
                [d, d, d], BF16, "proj_da")
            y = _attention(
                lay, functools.partial(_da_attn_kernel, lam_init=lam_init), q, k, v,
                [da_lambda[j], da_subln[j][None, :]],
                [_resident(da_lambda[j].shape), _resident((1, HEAD_W))],
                kv_heads=N_HEADS, q_block_w=HEAD_W, tq=DA_Q_TILE, need_ctx=need_ctx, name="attn_da")
            xa = _mixer_output(lay, _out_kernel, xa, [y], [lay.rows(d)], mod, g,
                               da_w_o[j].astype(BF16), [], out_tiles, "out_da")
        elif kind == 1:
            p, = _project(lay, _proj_hg_kernel, xa, mod, g, hg_w_in[j].astype(BF16),
                          [lb_table[:, i]], [_resident((2, d))], [5 * d], F32, "proj_hg")
            o_f = _hg_scan(lay, p, d, reverse=False)
            o_b = _hg_scan(lay, p, d, reverse=True)
            xa = _mixer_output(lay, _out_hg_kernel, xa, [o_f, o_b, p],
                               [lay.rows(d), lay.rows(d), lay.rows(d, 4)], mod, g,
                               hg_w_o[j].astype(BF16), [hg_norm[j][None, :]], out_tiles, "out_hg")
        else:
            kvw = d // GQA_GROUP
            w = gqa_w_qkv[j][:, gqa_cols].astype(BF16)
            q, k, v = _project(
                lay, functools.partial(_proj_gqa_kernel, q_scale=LOG2_E * HEAD_W ** -0.5),
                xa, mod, g, w,
                [gqa_cos, gqa_sin, gqa_q_norm[j][gqa_perm][None, :], gqa_k_norm[j][gqa_perm][None, :]],
                [lay.rope_spec(), lay.rope_spec(), _resident((1, HEAD_W)), _resident((1, HEAD_W))],
                [d, kvw, kvw], BF16, "proj_gqa")
            y = _attention(lay, _gqa_attn_kernel, q, k, v, [], [],
                           kv_heads=N_HEADS // GQA_GROUP, q_block_w=GQA_GROUP * HEAD_W,
                           tq=GQA_Q_TILE, need_ctx=need_ctx, name="attn_gqa")
            xa = _mixer_output(lay, _out_kernel, xa, [y], [lay.rows(d)], mod, g,
                               gqa_w_o[j].astype(BF16), [], out_tiles, "out_gqa")

        xa = _ffn_sublayer(lay, xa, mod, g, w_in[1], w_out[1], 1, out_tiles)
    return xa.reshape(batch, seq, d)
```

```python
import functools
import math

import numpy as np
import jax
import jax.numpy as jnp
from jax import lax
from jax.experimental import pallas as pl
from jax.experimental.pallas import tpu as pltpu

F32 = jnp.float32
BF16 = jnp.bfloat16

NORM_EPS = 1e-6
ROPE_THETA = 10000.0
GRID_W = 64
N_MOD = 9
N_MIXERS = 3
MACARON_WEIGHT = 0.5
LOG2_E = math.log2(math.e)
LANES = 128
SUBLANES = 8
MXU_W = 256
HEAD_W = 128
N_HEADS = 8
GQA_GROUP = 4
VMEM_LIMIT = 56 * 1024 * 1024

TOKEN_TILE = 512
DA_Q_TILE = 256
GQA_Q_TILE = 128
KEY_BLOCK = 512
HG_CHUNK = 64


def _params(*sem):
    return pltpu.CompilerParams(dimension_semantics=sem, vmem_limit_bytes=VMEM_LIMIT)


def _resident(shape, lead=()):
    block = (None,) * len(lead) + tuple(shape[len(lead):])
    index = tuple(lead) + (0,) * (len(shape) - len(lead))
    return pl.BlockSpec(block, lambda *_: index, pipeline_mode=pl.Buffered(1))


def _rms(x, g):
    return x * lax.rsqrt(jnp.mean(x * x, axis=-1, keepdims=True) + NORM_EPS) * g


def _modulated(x, g, shift, scale):
    return _rms(x, g) * (1.0 + scale) + shift


def _silu(x):
    return x * jax.nn.sigmoid(x)


def _mod_kernel(c_ref, w_ref, b_ref, o_ref):
    s = _silu(c_ref[...])
    o_ref[0] = jnp.dot(s, w_ref[0], preferred_element_type=F32,
                       precision=lax.Precision.HIGHEST) + b_ref[0]


def _modulation(c_rows, w_mod, b_mod):
    depth, d, n = w_mod.shape
    rows = c_rows.shape[0]
    tn = n // 8
    return pl.pallas_call(
        _mod_kernel,
        grid=(depth, n // tn),
        in_specs=[pl.BlockSpec((rows, d), lambda i, j: (0, 0)),
                  pl.BlockSpec((1, d, tn), lambda i, j: (i, 0, j)),
                  pl.BlockSpec((1, 1, tn), lambda i, j: (i, 0, j))],
        out_specs=pl.BlockSpec((1, rows, tn), lambda i, j: (i, 0, j)),
        out_shape=jax.ShapeDtypeStruct((depth, rows, n), F32),
        compiler_params=_params("arbitrary", "arbitrary"),
        name="modulation",
    )(c_rows, w_mod, b_mod.reshape(depth, 1, n))


class _Layout:
    def __init__(self, batch, seq, ctx, tm=TOKEN_TILE):
        self.batch, self.seq, self.ctx = batch, seq, ctx
        self.n_lat = batch * seq
        self.n_all = self.n_lat + batch * ctx
        assert seq % tm == 0 and (batch * ctx) % tm == 0
        self.tm = tm
        self.lat_tiles = self.n_lat // tm
        self.all_tiles = self.n_all // tm
        self.tiles_per_sample = seq // tm

    def mod_spec(self, d, layer):
        lat_tiles, tps, batch = self.lat_tiles, self.tiles_per_sample, self.batch
        return pl.BlockSpec((None, None, N_MOD, d),
                            lambda t: (layer, jnp.where(t < lat_tiles, t // tps, batch), 0, 0))

    def rope_spec(self):
        lat_tiles, tps = self.lat_tiles, self.tiles_per_sample
        return pl.BlockSpec((self.tm, LANES),
                            lambda t: (jnp.where(t < lat_tiles, t % tps, tps), 0))

    def rows(self, width, col=0):
        return pl.BlockSpec((self.tm, width), lambda t: (t, col))


def _ffn_chunks(d_ff):
    assert d_ff % MXU_W == 0
    first = (d_ff // MXU_W + 1) // 2 * MXU_W
    return [(0, first), (first, d_ff - first)]


def _ffn(x, mod_ref, g_ref, win_ref, wout_ref, which):
    mod_base, g_base = 6 * which, 4 * which
    d_ff = wout_ref.shape[0]
    shift = mod_ref[mod_base:mod_base + 1, :]
    scale = mod_ref[mod_base + 1:mod_base + 2, :]
    gate = mod_ref[mod_base + 2:mod_base + 3, :]
    h = _modulated(x, g_ref[g_base:g_base + 1, :], shift, scale).astype(BF16)
    chunks = _ffn_chunks(d_ff)
    hidden = [(jnp.dot(h, win_ref[:, lo:lo + n], preferred_element_type=F32),
               jnp.dot(h, win_ref[:, d_ff + lo:d_ff + lo + n], preferred_element_type=F32))
              for lo, n in chunks]
    y = None
    for (lo, n), (gt, up) in zip(chunks, hidden):
        a = (_silu(gt) * up).astype(BF16)
        part = jnp.dot(a, wout_ref[lo:lo + n, :], preferred_element_type=F32)
        y = part if y is None else y + part
    return x + (MACARON_WEIGHT * gate) * _rms(y, g_ref[g_base + 1:g_base + 2, :])


def _ffn_kernel(x_ref, mod_ref, g_ref, win_ref, wout_ref, o_ref):
    o_ref[...] = _ffn(x_ref[...], mod_ref, g_ref, win_ref, wout_ref, 0)


def _ffn_sublayer(lay, x, mod_all, norm_g, w_in, w_out, layer):
    d = x.shape[1]
    return pl.pallas_call(
        _ffn_kernel,
        grid=(lay.all_tiles,),
        in_specs=[lay.rows(d), lay.mod_spec(d, layer), _resident(norm_g.shape, (layer,)),
                  _resident(w_in.shape, (layer, 0)), _resident(w_out.shape, (layer, 0))],
        out_specs=lay.rows(d),
        out_shape=jax.ShapeDtypeStruct((lay.n_all, d), F32),
        compiler_params=_params("parallel"),
        name="ffn0",
    )(x, mod_all, norm_g, w_in, w_out)


def _mixer_input(x_ref, mod_ref, g_ref):
    return _modulated(x_ref[...], g_ref[2:3, :], mod_ref[3:4, :], mod_ref[4:5, :]).astype(BF16)


def _rope(x, cos, sin):
    return x * cos + pltpu.roll(x, LANES // 2, 1) * sin


def _proj_da_kernel(x_ref, mod_ref, g_ref, w_ref, cos_ref, sin_ref, q_ref, k_ref, v_ref, *, q_scale):
    h = _mixer_input(x_ref, mod_ref, g_ref)
    qkv = jnp.dot(h, w_ref[...], preferred_element_type=F32)
    cos, sin = cos_ref[...], sin_ref[...]
    width = q_ref.shape[1]
    for hb in range(width // HEAD_W):
        lo = hb * HEAD_W
        q_ref[:, lo:lo + HEAD_W] = (_rope(qkv[:, lo:lo + HEAD_W], cos, sin) * q_scale).astype(BF16)
        k_ref[:, lo:lo + HEAD_W] = _rope(qkv[:, width + lo:width + lo + HEAD_W], cos, sin).astype(BF16)
    v_ref[...] = qkv[:, 2 * width:].astype(BF16)


def _proj_gqa_kernel(x_ref, mod_ref, g_ref, w_ref, cos_ref, sin_ref, qn_ref, kn_ref,
                     q_ref, k_ref, v_ref, *, q_scale):
    h = _mixer_input(x_ref, mod_ref, g_ref)
    qkv = jnp.dot(h, w_ref[...], preferred_element_type=F32)
    cos, sin = cos_ref[...], sin_ref[...]
    qw, kw = q_ref.shape[1], k_ref.shape[1]
    for hb in range(qw // HEAD_W):
        lo = hb * HEAD_W
        xq = _rms(qkv[:, lo:lo + HEAD_W], qn_ref[...])
        q_ref[:, lo:lo + HEAD_W] = (_rope(xq, cos, sin) * q_scale).astype(BF16)
    for hb in range(kw // HEAD_W):
        lo = hb * HEAD_W
        xk = _rms(qkv[:, qw + lo:qw + lo + HEAD_W], kn_ref[...])
        k_ref[:, lo:lo + HEAD_W] = _rope(xk, cos, sin).astype(BF16)
    v_ref[...] = qkv[:, qw + kw:].astype(BF16)


def _proj_hg_kernel(x_ref, mod_ref, g_ref, w_ref, lb_ref, o_ref):
    h = _mixer_input(x_ref, mod_ref, g_ref)
    p = jnp.dot(h, w_ref[...], preferred_element_type=F32)
    d = x_ref.shape[1]
    o_ref[:, 0:d] = _silu(p[:, 0:d])
    for k in range(2):
        lb = lb_ref[k:k + 1, :]
        f = lb + (1.0 - lb) * jax.nn.sigmoid(p[:, (1 + k) * d:(2 + k) * d])
        o_ref[:, (1 + k) * d:(2 + k) * d] = jnp.log2(f)
    o_ref[:, 3 * d:] = p[:, 3 * d:]


def _project(lay, kern, x, mod_all, norm_g, w, layer, j, extra, extra_specs, out_widths, out_dtype,
             name):
    d = x.shape[1]
    outs = pl.pallas_call(
        kern,
        grid=(lay.all_tiles,),
        in_specs=[lay.rows(d), lay.mod_spec(d, layer), _resident(norm_g.shape, (layer,)),
                  _resident(w.shape, (j,))] + extra_specs,
        out_specs=[lay.rows(wd) for wd in out_widths],
        out_shape=[jax.ShapeDtypeStruct((lay.n_all, wd), out_dtype) for wd in out_widths],
        compiler_params=_params("parallel"),
        name=name,
    )(x, mod_all, norm_g, w, *extra)
    return outs


def _nt_dot(a, b):
    return lax.dot_general(a, b, (((1,), (1,)), ((), ())), preferred_element_type=F32)


def _stage_keys(kl_ref, kc_ref, vl_ref, vc_ref, k_s, v_s):
    seq = kl_ref.shape[0]
    k_s[0:seq, :] = kl_ref[...]
    k_s[seq:, :] = kc_ref[...]
    v_s[0:seq, 0:HEAD_W] = vl_ref[...]
    v_s[seq:, 0:HEAD_W] = vc_ref[...]
    v_s[:, HEAD_W:] = jnp.ones((v_s.shape[0], HEAD_W), BF16)


def _key_blocks(start, stop):
    return [(s, min(KEY_BLOCK, stop - s)) for s in range(start, stop, KEY_BLOCK)]


def _softmax_pv(q, k_s, v_s, blocks):
    def scores(blk):
        return _nt_dot(q, k_s[blk[0]:blk[0] + blk[1], :])

    m = acc = None
    s_next = scores(blocks[0])
    for i, (start, size) in enumerate(blocks):
        s = s_next
        if i + 1 < len(blocks):
            s_next = scores(blocks[i + 1])
        m_new = jnp.max(s, axis=-1, keepdims=True)
        if m is not None:
            m_new = jnp.maximum(m, m_new)
        e = jnp.exp2(s - m_new).astype(BF16)
        pv = jnp.dot(e, v_s[start:start + size, :], preferred_element_type=F32)
        acc = pv if acc is None else acc * jnp.exp2(m - m_new) + pv
        m = m_new
    return acc[:, :HEAD_W] / acc[:, HEAD_W:]


def _da_attend(q, k_s, v_s, blocks, lam, gain):
    tq = q.shape[0]
    lane = lax.broadcasted_iota(jnp.int32, (1, LANES), 1)
    first_map = (lane % (LANES // 2)) < (LANES // 4)
    zero = jnp.zeros_like(q)
    qq = jnp.concatenate([jnp.where(first_map, q, zero), jnp.where(first_map, zero, q)], axis=0)
    p = _softmax_pv(qq, k_s, v_s, blocks)
    return _rms(p[:tq] - lam * p[tq:], gain)


def _da_attn_kernel(q_ref, kl_ref, kc_ref, vl_ref, vc_ref, lam_ref, g_ref, o_ref, k_s, v_s,
                    *, nq, lam_init):
    lp = lam_ref[...]
    lam = (jnp.exp(jnp.sum(lp[0:1] * lp[1:2], axis=-1, keepdims=True))
           - jnp.exp(jnp.sum(lp[2:3] * lp[3:4], axis=-1, keepdims=True)) + lam_init)
    gain = g_ref[...] * (1.0 - lam_init)
    i = pl.program_id(2)
    seq = kl_ref.shape[0]

    @pl.when(i == 0)
    def _():
        _stage_keys(kl_ref, kc_ref, vl_ref, vc_ref, k_s, v_s)

    total = k_s.shape[0]

    @pl.when(i < nq)
    def _():
        o_ref[...] = _da_attend(q_ref[...], k_s, v_s, _key_blocks(0, total), lam, gain).astype(BF16)

    @pl.when(i >= nq)
    def _():
        o_ref[...] = _da_attend(q_ref[...], k_s, v_s, _key_blocks(seq, total), lam, gain).astype(BF16)


def _gqa_attend(q_ref, k_s, v_s, blocks, o_ref):
    tq = q_ref.shape[0]
    qs = jnp.concatenate([q_ref[:, g * HEAD_W:(g + 1) * HEAD_W] for g in range(GQA_GROUP)], axis=0)
    o = _softmax_pv(qs, k_s, v_s, blocks)
    for g in range(GQA_GROUP):
        o_ref[:, g * HEAD_W:(g + 1) * HEAD_W] = o[g * tq:(g + 1) * tq].astype(BF16)


def _gqa_attn_kernel(q_ref, kl_ref, kc_ref, vl_ref, vc_ref, o_ref, k_s, v_s, *, nq):
    i = pl.program_id(2)
    seq = kl_ref.shape[0]

    @pl.when(i == 0)
    def _():
        _stage_keys(kl_ref, kc_ref, vl_ref, vc_ref, k_s, v_s)

    total = k_s.shape[0]

    @pl.when(i < nq)
    def _():
        _gqa_attend(q_ref, k_s, v_s, _key_blocks(0, total), o_ref)

    @pl.when(i >= nq)
    def _():
        _gqa_attend(q_ref, k_s, v_s, _key_blocks(seq, total), o_ref)


def _attention(lay, kern, q, k, v, extra, extra_specs, *, kv_heads, q_block_w, tq, need_ctx, name):
    batch, seq, ctx, n_lat = lay.batch, lay.seq, lay.ctx, lay.n_lat
    assert seq % tq == 0 and ctx % tq == 0 and n_lat % ctx == 0
    nq, nqc = seq // tq, (ctx // tq if need_ctx else 0)
    lat_rows, ctx_base = n_lat // tq, n_lat // ctx

    def q_map(b, h, i):
        return (jnp.where(i < nq, b * nq + i, lat_rows + b * (ctx // tq) + (i - nq)), h)

    kv_lat = pl.BlockSpec((seq, HEAD_W), lambda b, h, i: (b, h))
    kv_ctx = pl.BlockSpec((ctx, HEAD_W), lambda b, h, i: (ctx_base + b, h))
    n_rows = lay.n_all if need_ctx else n_lat
    return pl.pallas_call(
        functools.partial(kern, nq=nq),
        grid=(batch, kv_heads, nq + nqc),
        in_specs=[pl.BlockSpec((tq, q_block_w), q_map), kv_lat, kv_ctx, kv_lat, kv_ctx] + extra_specs,
        out_specs=pl.BlockSpec((tq, q_block_w), q_map),
        out_shape=jax.ShapeDtypeStruct((n_rows, q.shape[1]), BF16),
        scratch_shapes=[pltpu.VMEM((seq + ctx, HEAD_W), BF16),
                        pltpu.VMEM((seq + ctx, 2 * HEAD_W), BF16)],
        compiler_params=_params("parallel", "parallel", "arbitrary"),
        name=name,
    )(q, k, k, v, v, *extra)


def _hg_scan_kernel(q_ref, lf_ref, v_ref, o_ref, st_ref, *, reverse):
    c = q_ref.shape[0]
    j = pl.program_id(1)

    @pl.when(j == 0)
    def _():
        st_ref[...] = jnp.zeros_like(st_ref)

    row = lax.broadcasted_iota(jnp.int32, (c, c), 0)
    col = lax.broadcasted_iota(jnp.int32, (c, c), 1)
    later, earlier = (col, row) if reverse else (row, col)
    tri = (earlier <= later).astype(F32)
    cum_all = jnp.dot(tri, lf_ref[...], preferred_element_type=F32, precision=lax.Precision.HIGHEST)

    levels = []
    half = c // 2
    while half >= 1:
        same = (row // (2 * half)) == (col // (2 * half))
        q_side = (later % (2 * half)) >= half
        k_side = (earlier % (2 * half)) < half
        levels.append((half, same & q_side & k_side))
        half //= 2
    nv = c // SUBLANES
    t_loc = lax.broadcasted_iota(jnp.int32, (nv, SUBLANES, HEAD_W), 1)

    def boundary(cum, half):
        b_row = half if reverse else half - 1
        if 2 * half >= SUBLANES:
            c3 = cum.reshape(c // (2 * half), 2 * half, HEAD_W)
            return jnp.broadcast_to(c3[:, b_row:b_row + 1, :], c3.shape).reshape(c, HEAD_W)
        c3 = cum.reshape(nv, SUBLANES, HEAD_W)
        ref = None
        for blk in range(SUBLANES // (2 * half)):
            r = blk * 2 * half + b_row
            cand = jnp.broadcast_to(c3[:, r:r + 1, :], c3.shape)
            ref = cand if ref is None else jnp.where(t_loc >= blk * 2 * half, cand, ref)
        return ref.reshape(c, HEAD_W)

    heads = range(q_ref.shape[1] // HEAD_W)
    end = 0 if reverse else c - 1

    staged = []
    for h in heads:
        sl = slice(h * HEAD_W, (h + 1) * HEAD_W)
        q, lf, v = q_ref[:, sl], lf_ref[:, sl], v_ref[:, sl]
        cum = cum_all[:, sl]
        f = jnp.exp2(lf)
        k = 1.0 - f
        v_b, k_b = v.astype(BF16), k.astype(BF16)
        att = jnp.where(row == col, _nt_dot(q.astype(BF16), k_b), 0.0)
        for half, mask in levels:
            if half == 1:
                qt, kt = (q * f).astype(BF16), k_b
            else:
                ref = boundary(cum, half)
                qt = (q * jnp.exp2(jnp.minimum(cum - ref, 0.0))).astype(BF16)
                kt = (k * jnp.exp2(jnp.minimum(ref - cum, 0.0))).astype(BF16)
            att = jnp.where(mask, _nt_dot(qt, kt), att)
        total = cum[end:end + 1, :]
        st = st_ref[h]
        inter = _nt_dot((q * jnp.exp2(cum)).astype(BF16), st.astype(BF16))
        kd = (k * jnp.exp2(total - cum)).astype(BF16)
        upd = lax.dot_general(v_b, kd, (((0,), (0,)), ((), ())), preferred_element_type=F32)
        st_ref[h] = st * jnp.exp2(total) + upd
        staged.append((sl, att.astype(BF16), v_b, inter))

    for sl, att_b, v_b, inter in staged:
        o_ref[:, sl] = jnp.dot(att_b, v_b, preferred_element_type=F32) + inter


def _hg_scan(lay, p, d, reverse):
    batch, seq, ctx, n_lat = lay.batch, lay.seq, lay.ctx, lay.n_lat
    c = HG_CHUNK
    assert seq % c == 0 and ctx % c == 0
    nlc, ncc = seq // c, ctx // c
    ctx_base = n_lat // c

    def blk(b, j):
        c_ctx = (ncc - 1 - j) if reverse else j
        c_lat = (nlc - 1 - (j - ncc)) if reverse else (j - ncc)
        return jnp.where(j < ncc, ctx_base + b * ncc + c_ctx, b * nlc + c_lat)

    def col_spec(colblk):
        return pl.BlockSpec((c, d), lambda b, j: (blk(b, j), colblk))

    return pl.pallas_call(
        functools.partial(_hg_scan_kernel, reverse=reverse),
        grid=(batch, ncc + nlc),
        in_specs=[col_spec(0), col_spec(2 if reverse else 1), col_spec(3)],
        out_specs=col_spec(0),
        out_shape=jax.ShapeDtypeStruct((lay.n_all, d), F32),
        scratch_shapes=[pltpu.VMEM((d // HEAD_W, HEAD_W, HEAD_W), F32)],
        compiler_params=_params("arbitrary", "arbitrary"),
        name="hg_scan_bwd" if reverse else "hg_scan_fwd",
    )(p, p, p)


def _post_mixer(x, y_b, mod_ref, g_ref, wo_ref, win_ref, wout_ref):
    y = jnp.dot(y_b, wo_ref[...], preferred_element_type=F32)
    x = x + mod_ref[5:6, :] * _rms(y, g_ref[3:4, :])
    return _ffn(x, mod_ref, g_ref, win_ref, wout_ref, 1)


def _post_kernel(x_ref, y_ref, mod_ref, g_ref, wo_ref, win_ref, wout_ref, o_ref):
    o_ref[...] = _post_mixer(x_ref[...], y_ref[...], mod_ref, g_ref, wo_ref, win_ref, wout_ref)


def _post_hg_kernel(x_ref, of_ref, ob_ref, gate_ref, hn_ref, mod_ref, g_ref, wo_ref, win_ref,
                    wout_ref, o_ref):
    o = of_ref[...] + ob_ref[...]
    gate = gate_ref[...]
    parts = []
    for h in range(o.shape[1] // HEAD_W):
        sl = slice(h * HEAD_W, (h + 1) * HEAD_W)
        parts.append((_rms(o[:, sl], hn_ref[...]) * _silu(gate[:, sl])).astype(BF16))
    o_ref[...] = _post_mixer(x_ref[...], jnp.concatenate(parts, axis=1), mod_ref, g_ref, wo_ref,
                             win_ref, wout_ref)


def _post_sublayer(lay, kern, x, ys, y_specs, extra, extra_specs, mod_all, norm_g, w_o, w_in, w_out,
                   layer, j, n_tiles, name):
    d = x.shape[1]
    return pl.pallas_call(
        kern,
        grid=(n_tiles,),
        in_specs=[lay.rows(d)] + y_specs + extra_specs
                 + [lay.mod_spec(d, layer), _resident(norm_g.shape, (layer,)),
                    _resident(w_o.shape, (j,)), _resident(w_in.shape, (layer, 1)),
                    _resident(w_out.shape, (layer, 1))],
        out_specs=lay.rows(d),
        out_shape=jax.ShapeDtypeStruct((n_tiles * lay.tm, d), F32),
        compiler_params=_params("parallel"),
        name=name,
    )(x, *ys, *extra, mod_all, norm_g, w_o, w_in, w_out)


def _rope_tables(rows, head_dim, n_id_rows, maps):
    pairs = head_dim // 4
    inv_freq = jnp.power(ROPE_THETA, -jnp.arange(pairs, dtype=F32) / pairs)
    r = jnp.repeat(jnp.arange(rows, dtype=F32), GRID_W)
    col = jnp.tile(jnp.arange(GRID_W, dtype=F32), rows)
    ang = jnp.concatenate([r[:, None] * inv_freq, col[:, None] * inv_freq], axis=-1)
    cos, sin = jnp.cos(ang), jnp.sin(ang)
    cos = jnp.tile(cos, (1, 2 * maps))
    sin = jnp.concatenate([-jnp.tile(sin, (1, maps)), jnp.tile(sin, (1, maps))], axis=-1)
    cos = jnp.concatenate([cos, jnp.ones((n_id_rows, LANES), F32)], axis=0)
    sin = jnp.concatenate([sin, jnp.zeros((n_id_rows, LANES), F32)], axis=0)
    return cos, sin


def _da_head_perm():
    p = np.arange(2)[:, None, None]
    m = np.arange(2)[None, :, None]
    j = np.arange(HEAD_W // 4)[None, None, :]
    return (m * (HEAD_W // 2) + 2 * j + p).reshape(-1)


def _gqa_head_perm():
    p = np.arange(2)[:, None]
    j = np.arange(HEAD_W // 2)[None, :]
    return (2 * j + p).reshape(-1)


def _permute_heads(n_blocks, perm, n_tail):
    idx = (np.arange(n_blocks)[:, None] * HEAD_W + perm[None, :]).reshape(-1)
    return np.concatenate([idx, n_blocks * HEAD_W + np.arange(n_tail)])


def kernel(x, c, ctx, c_ctx, w_mod, b_mod, norm_g, ffn_w_in, ffn_w_out, da_w_qkv, da_lambda, da_subln, da_w_o, hg_w_in, hg_lower_bound, hg_norm, hg_w_o, gqa_w_qkv, gqa_q_norm, gqa_k_norm, gqa_w_o):
    batch, seq, d = x.shape
    n_ctx = ctx.shape[1]
    depth = w_mod.shape[0]
    assert d == N_HEADS * HEAD_W
    lay = _Layout(batch, seq, n_ctx)
    lay_hg = _Layout(batch, seq, n_ctx, TOKEN_TILE // 2)
    rows = seq // GRID_W

    pad = (-(batch + 1)) % SUBLANES
    c_rows = jnp.concatenate([c, c_ctx[None, :], jnp.zeros((pad, d), F32)], axis=0)
    mod_all = _modulation(c_rows, w_mod, b_mod)[:, :batch + 1].reshape(depth, batch + 1, N_MOD, d)

    w_in, w_out = ffn_w_in.astype(BF16), ffn_w_out.astype(BF16)
    da_cos, da_sin = _rope_tables(rows, HEAD_W // 2, lay.tm, 2)
    gqa_cos, gqa_sin = _rope_tables(rows, HEAD_W, lay.tm, 1)
    da_w = da_w_qkv[:, :, _permute_heads(2 * N_HEADS, _da_head_perm(), d)].astype(BF16)
    gqa_perm = _gqa_head_perm()
    kvw = d // GQA_GROUP
    gqa_w = gqa_w_qkv[:, :, _permute_heads(N_HEADS + N_HEADS // GQA_GROUP, gqa_perm, kvw)].astype(BF16)
    gqa_qn, gqa_kn = gqa_q_norm[:, None, gqa_perm], gqa_k_norm[:, None, gqa_perm]
    da_wo, hg_wo, gqa_wo = da_w_o.astype(BF16), hg_w_o.astype(BF16), gqa_w_o.astype(BF16)
    hg_w = hg_w_in.astype(BF16)
    lb_table = jnp.cumsum(jax.nn.softmax(hg_lower_bound.astype(F32), axis=1), axis=1)
    lb_table = jnp.swapaxes(lb_table - lb_table[:, :1], 0, 1)
    da_gain, hg_gain = da_subln[:, None, :], hg_norm[:, None, :]

    xa = jnp.concatenate([x.reshape(batch * seq, d), ctx.reshape(batch * n_ctx, d)], axis=0)
    for i in range(depth):
        kind, j = i % N_MIXERS, i // N_MIXERS
        need_ctx = i < depth - 1
        xa = _ffn_sublayer(lay, xa, mod_all, norm_g, w_in, w_out, i)
        post = functools.partial(_post_sublayer, mod_all=mod_all, norm_g=norm_g, w_in=w_in,
                                 w_out=w_out, layer=i, j=j)

        if kind == 0:
            lam_init = 0.8 - 0.6 * math.exp(-0.3 * i)
            q, k, v = _project(
                lay, functools.partial(_proj_da_kernel, q_scale=LOG2_E * (HEAD_W // 2) ** -0.5),
                xa, mod_all, norm_g, da_w, i, j, [da_cos, da_sin], [lay.rope_spec(), lay.rope_spec()],
                [d, d, d], BF16, "proj_da")
            y = _attention(
                lay, functools.partial(_da_attn_kernel, lam_init=lam_init), q, k, v,
                [da_lambda, da_gain],
                [_resident(da_lambda.shape, (j,)), _resident(da_gain.shape, (j,))],
                kv_heads=N_HEADS, q_block_w=HEAD_W, tq=DA_Q_TILE, need_ctx=need_ctx, name="attn_da")
            xa = post(lay, _post_kernel, xa, [y], [lay.rows(d)], [], [], w_o=da_wo,
                      n_tiles=lay.all_tiles if need_ctx else lay.lat_tiles, name="post_da")
        elif kind == 1:
            p, = _project(lay, _proj_hg_kernel, xa, mod_all, norm_g, hg_w, i, j,
                          [lb_table], [_resident(lb_table.shape, (i,))], [5 * d], F32, "proj_hg")
            o_f = _hg_scan(lay, p, d, reverse=False)
            o_b = _hg_scan(lay, p, d, reverse=True)
            xa = post(lay_hg, _post_hg_kernel, xa, [o_f, o_b, p],
                      [lay_hg.rows(d), lay_hg.rows(d), lay_hg.rows(d, 4)],
                      [hg_gain], [_resident(hg_gain.shape, (j,))], w_o=hg_wo,
                      n_tiles=lay_hg.all_tiles if need_ctx else lay_hg.lat_tiles, name="post_hg")
        else:
            q, k, v = _project(
                lay, functools.partial(_proj_gqa_kernel, q_scale=LOG2_E * HEAD_W ** -0.5),
                xa, mod_all, norm_g, gqa_w, i, j,
                [gqa_cos, gqa_sin, gqa_qn, gqa_kn],
                [lay.rope_spec(), lay.rope_spec(), _resident(gqa_qn.shape, (j,)),
                 _resident(gqa_kn.shape, (j,))],
                [d, kvw, kvw], BF16, "proj_gqa")
            y = _attention(lay, _gqa_attn_kernel, q, k, v, [], [],
                           kv_heads=N_HEADS // GQA_GROUP, q_block_w=GQA_GROUP * HEAD_W,
                           tq=GQA_Q_TILE, need_ctx=need_ctx, name="attn_gqa")
            xa = post(lay, _post_kernel, xa, [y], [lay.rows(d)], [], [], w_o=gqa_wo,
                      n_tiles=lay.all_tiles if need_ctx else lay.lat_tiles, name="post_gqa")
    return xa.reshape(batch, seq, d)
```

```python
import functools
import math

import numpy as np
import jax
import jax.numpy as jnp
from jax import lax
from jax.experimental import pallas as pl
from jax.experimental.pallas import tpu as pltpu

F32 = jnp.float32
BF16 = jnp.bfloat16

NORM_EPS = 1e-6
ROPE_THETA = 10000.0
GRID_W = 64
N_MOD = 9
N_MIXERS = 3
MACARON_WEIGHT = 0.5
LOG2_E = math.log2(math.e)
LANES = 128
SUBLANES = 8
MXU_W = 256
HEAD_W = 128
N_HEADS = 8
GQA_GROUP = 4
VMEM_LIMIT = 56 * 1024 * 1024

TOKEN_TILE = 512
DA_Q_TILE = 256
GQA_Q_TILE = 128
KEY_BLOCK = 512
ATTN_KV_PER_STEP = 2
SCORE_LOOKAHEAD = 1
HG_CHUNK = 64
HG_CHUNKS_PER_STEP = 4


def _params(*sem):
    return pltpu.CompilerParams(dimension_semantics=sem, vmem_limit_bytes=VMEM_LIMIT)


def _resident(shape, lead=()):
    block = (None,) * len(lead) + tuple(shape[len(lead):])
    index = tuple(lead) + (0,) * (len(shape) - len(lead))
    return pl.BlockSpec(block, lambda *_: index, pipeline_mode=pl.Buffered(1))


def _rms(x, g):
    return x * lax.rsqrt(jnp.mean(x * x, axis=-1, keepdims=True) + NORM_EPS) * g


def _modulated(x, g, shift, scale):
    return _rms(x, g) * (1.0 + scale) + shift


def _silu(x):
    return x * jax.nn.sigmoid(x)


def _mod_kernel(c_ref, w_ref, b_ref, o_ref):
    s = _silu(c_ref[...])
    o_ref[0] = jnp.dot(s, w_ref[0], preferred_element_type=F32,
                       precision=lax.Precision.HIGHEST) + b_ref[0]


def _modulation(c_rows, w_mod, b_mod):
    depth, d, n = w_mod.shape
    rows = c_rows.shape[0]
    tn = n // 8
    return pl.pallas_call(
        _mod_kernel,
        grid=(depth, n // tn),
        in_specs=[pl.BlockSpec((rows, d), lambda i, j: (0, 0)),
                  pl.BlockSpec((1, d, tn), lambda i, j: (i, 0, j)),
                  pl.BlockSpec((1, 1, tn), lambda i, j: (i, 0, j))],
        out_specs=pl.BlockSpec((1, rows, tn), lambda i, j: (i, 0, j)),
        out_shape=jax.ShapeDtypeStruct((depth, rows, n), F32),
        compiler_params=_params("arbitrary", "arbitrary"),
        name="modulation",
    )(c_rows, w_mod, b_mod.reshape(depth, 1, n))


class _Layout:
    def __init__(self, batch, seq, ctx, tm=TOKEN_TILE):
        self.batch, self.seq, self.ctx = batch, seq, ctx
        self.n_lat = batch * seq
        self.n_all = self.n_lat + batch * ctx
        assert seq % tm == 0 and (batch * ctx) % tm == 0
        self.tm = tm
        self.lat_tiles = self.n_lat // tm
        self.all_tiles = self.n_all // tm
        self.tiles_per_sample = seq // tm

    def mod_spec(self, d, layer):
        lat_tiles, tps, batch = self.lat_tiles, self.tiles_per_sample, self.batch
        return pl.BlockSpec((None, None, N_MOD, d),
                            lambda t: (layer, jnp.where(t < lat_tiles, t // tps, batch), 0, 0))

    def rope_spec(self):
        lat_tiles, tps = self.lat_tiles, self.tiles_per_sample
        return pl.BlockSpec((self.tm, LANES),
                            lambda t: (jnp.where(t < lat_tiles, t % tps, tps), 0))

    def rows(self, width, col=0):
        return pl.BlockSpec((self.tm, width), lambda t: (t, col))


def _ffn_chunks(d_ff):
    assert d_ff % MXU_W == 0
    first = (d_ff // MXU_W + 1) // 2 * MXU_W
    return [(0, first), (first, d_ff - first)]


def _ffn(x, mod_ref, g_ref, win_ref, wout_ref, which):
    mod_base, g_base = 6 * which, 4 * which
    d_ff = wout_ref.shape[0]
    shift = mod_ref[mod_base:mod_base + 1, :]
    scale = mod_ref[mod_base + 1:mod_base + 2, :]
    gate = mod_ref[mod_base + 2:mod_base + 3, :]
    h = _modulated(x, g_ref[g_base:g_base + 1, :], shift, scale).astype(BF16)
    chunks = _ffn_chunks(d_ff)
    hidden = [(jnp.dot(h, win_ref[:, lo:lo + n], preferred_element_type=F32),
               jnp.dot(h, win_ref[:, d_ff + lo:d_ff + lo + n], preferred_element_type=F32))
              for lo, n in chunks]
    y = None
    for (lo, n), (gt, up) in zip(chunks, hidden):
        a = (_silu(gt) * up).astype(BF16)
        part = jnp.dot(a, wout_ref[lo:lo + n, :], preferred_element_type=F32)
        y = part if y is None else y + part
    return x + (MACARON_WEIGHT * gate) * _rms(y, g_ref[g_base + 1:g_base + 2, :])


def _ffn_kernel(x_ref, mod_ref, g_ref, win_ref, wout_ref, o_ref):
    o_ref[...] = _ffn(x_ref[...], mod_ref, g_ref, win_ref, wout_ref, 0)


def _ffn_first_kernel(xl_ref, xc_ref, mod_ref, g_ref, win_ref, wout_ref, o_ref, *, lat_tiles):
    x = jnp.where(pl.program_id(0) < lat_tiles, xl_ref[...], xc_ref[...])
    o_ref[...] = _ffn(x, mod_ref, g_ref, win_ref, wout_ref, 0)


def _ffn_sublayer(lay, xs, mod_all, norm_g, w_in, w_out, layer):
    d = xs[0].shape[1]
    if len(xs) == 1:
        kern, x_specs = _ffn_kernel, [lay.rows(d)]
    else:
        lat_tiles, tm = lay.lat_tiles, lay.tm
        kern = functools.partial(_ffn_first_kernel, lat_tiles=lat_tiles)
        x_specs = [pl.BlockSpec((tm, d), lambda t: (jnp.minimum(t, lat_tiles - 1), 0)),
                   pl.BlockSpec((tm, d), lambda t: (jnp.maximum(t - lat_tiles, 0), 0))]
    return pl.pallas_call(
        kern,
        grid=(lay.all_tiles,),
        in_specs=x_specs + [lay.mod_spec(d, layer), _resident(norm_g.shape, (layer,)),
                            _resident(w_in.shape, (layer, 0)), _resident(w_out.shape, (layer, 0))],
        out_specs=lay.rows(d),
        out_shape=jax.ShapeDtypeStruct((lay.n_all, d), F32),
        compiler_params=_params("parallel"),
        name="ffn0",
    )(*xs, mod_all, norm_g, w_in, w_out)


def _mixer_input(x_ref, mod_ref, g_ref):
    return _modulated(x_ref[...], g_ref[2:3, :], mod_ref[3:4, :], mod_ref[4:5, :]).astype(BF16)


def _rope(x, cos, sin):
    return x * cos + pltpu.roll(x, LANES // 2, 1) * sin


def _proj_da_kernel(x_ref, mod_ref, g_ref, w_ref, cos_ref, sin_ref, q_ref, k_ref, v_ref, *, q_scale):
    h = _mixer_input(x_ref, mod_ref, g_ref)
    qkv = jnp.dot(h, w_ref[...], preferred_element_type=F32)
    cos, sin = cos_ref[...], sin_ref[...]
    width = q_ref.shape[1]
    for hb in range(width // HEAD_W):
        lo = hb * HEAD_W
        q_ref[:, lo:lo + HEAD_W] = (_rope(qkv[:, lo:lo + HEAD_W], cos, sin) * q_scale).astype(BF16)
        k_ref[:, lo:lo + HEAD_W] = _rope(qkv[:, width + lo:width + lo + HEAD_W], cos, sin).astype(BF16)
    v_ref[...] = qkv[:, 2 * width:].astype(BF16)


def _proj_gqa_kernel(x_ref, mod_ref, g_ref, w_ref, cos_ref, sin_ref, qn_ref, kn_ref,
                     q_ref, k_ref, v_ref, *, q_scale):
    h = _mixer_input(x_ref, mod_ref, g_ref)
    qkv = jnp.dot(h, w_ref[...], preferred_element_type=F32)
    cos, sin = cos_ref[...], sin_ref[...]
    qw, kw = q_ref.shape[1], k_ref.shape[1]
    for hb in range(qw // HEAD_W):
        lo = hb * HEAD_W
        xq = _rms(qkv[:, lo:lo + HEAD_W], qn_ref[...])
        q_ref[:, lo:lo + HEAD_W] = (_rope(xq, cos, sin) * q_scale).astype(BF16)
    for hb in range(kw // HEAD_W):
        lo = hb * HEAD_W
        xk = _rms(qkv[:, qw + lo:qw + lo + HEAD_W], kn_ref[...])
        k_ref[:, lo:lo + HEAD_W] = _rope(xk, cos, sin).astype(BF16)
    v_ref[...] = qkv[:, qw + kw:].astype(BF16)


def _proj_hg_kernel(x_ref, mod_ref, g_ref, w_ref, lb_ref, o_ref):
    h = _mixer_input(x_ref, mod_ref, g_ref)
    p = jnp.dot(h, w_ref[...], preferred_element_type=F32)
    d = x_ref.shape[1]
    o_ref[:, 0:d] = _silu(p[:, 0:d])
    for k in range(2):
        lb = lb_ref[k:k + 1, :]
        f = lb + (1.0 - lb) * jax.nn.sigmoid(p[:, (1 + k) * d:(2 + k) * d])
        o_ref[:, (1 + k) * d:(2 + k) * d] = jnp.log2(f)
    o_ref[:, 3 * d:] = p[:, 3 * d:]


def _project(lay, kern, x, mod_all, norm_g, w, layer, j, extra, extra_specs, out_widths, out_dtype,
             name):
    d = x.shape[1]
    outs = pl.pallas_call(
        kern,
        grid=(lay.all_tiles,),
        in_specs=[lay.rows(d), lay.mod_spec(d, layer), _resident(norm_g.shape, (layer,)),
                  _resident(w.shape, (j,))] + extra_specs,
        out_specs=[lay.rows(wd) for wd in out_widths],
        out_shape=[jax.ShapeDtypeStruct((lay.n_all, wd), out_dtype) for wd in out_widths],
        compiler_params=_params("parallel"),
        name=name,
    )(x, mod_all, norm_g, w, *extra)
    return outs


def _nt_dot(a, b):
    return lax.dot_general(a, b, (((1,), (1,)), ((), ())), preferred_element_type=F32)


def _stage_keys(kl_ref, kc_ref, vl_ref, vc_ref, k_s, v_s):
    seq = kl_ref.shape[0]
    for h in range(k_s.shape[0]):
        sl = slice(h * HEAD_W, (h + 1) * HEAD_W)
        k_s[h, 0:seq, :] = kl_ref[:, sl]
        k_s[h, seq:, :] = kc_ref[:, sl]
        v_s[h, 0:seq, 0:HEAD_W] = vl_ref[:, sl]
        v_s[h, seq:, 0:HEAD_W] = vc_ref[:, sl]
        v_s[h, :, HEAD_W:] = jnp.ones((v_s.shape[1], HEAD_W), BF16)


def _key_blocks(start, stop):
    return [(s, min(KEY_BLOCK, stop - s)) for s in range(start, stop, KEY_BLOCK)]


def _softmax_pv(queries, k_s, v_s, blocks, finish):
    steps = [(h, blk) for h in range(len(queries)) for blk in blocks]

    def scores(step):
        h, (start, size) = step
        return _nt_dot(queries[h], k_s[h, start:start + size, :])

    m = acc = None
    ahead = [scores(step) for step in steps[:SCORE_LOOKAHEAD]]
    for n, (h, (start, size)) in enumerate(steps):
        s = ahead.pop(0)
        if n + SCORE_LOOKAHEAD < len(steps):
            ahead.append(scores(steps[n + SCORE_LOOKAHEAD]))
        m_new = jnp.max(s, axis=-1, keepdims=True)
        if m is not None:
            m_new = jnp.maximum(m, m_new)
        e = jnp.exp2(s - m_new).astype(BF16)
        pv = jnp.dot(e, v_s[h, start:start + size, :], preferred_element_type=F32)
        acc = pv if acc is None else acc * jnp.exp2(m - m_new) + pv
        m = m_new
        if (start, size) == blocks[-1]:
            finish(h, acc[:, :HEAD_W] / acc[:, HEAD_W:])
            m = acc = None


def _da_attend(q_ref, k_s, v_s, blocks, lam, gain, o_ref):
    tq = q_ref.shape[0]
    lane = lax.broadcasted_iota(jnp.int32, (1, LANES), 1)
    first_map = (lane % (LANES // 2)) < (LANES // 4)
    queries = []
    for h in range(k_s.shape[0]):
        q = q_ref[:, h * HEAD_W:(h + 1) * HEAD_W]
        zero = jnp.zeros_like(q)
        queries.append(jnp.concatenate([jnp.where(first_map, q, zero),
                                        jnp.where(first_map, zero, q)], axis=0))

    def finish(h, p):
        o_ref[:, h * HEAD_W:(h + 1) * HEAD_W] = _rms(p[:tq] - lam * p[tq:], gain).astype(BF16)

    _softmax_pv(queries, k_s, v_s, blocks, finish)


def _da_attn_kernel(q_ref, kl_ref, kc_ref, vl_ref, vc_ref, lam_ref, g_ref, o_ref, k_s, v_s,
                    *, nq, lam_init):
    lp = lam_ref[...]
    lam = (jnp.exp(jnp.sum(lp[0:1] * lp[1:2], axis=-1, keepdims=True))
           - jnp.exp(jnp.sum(lp[2:3] * lp[3:4], axis=-1, keepdims=True)) + lam_init)
    gain = g_ref[...] * (1.0 - lam_init)
    i = pl.program_id(2)
    seq = kl_ref.shape[0]

    @pl.when(i == 0)
    def _():
        _stage_keys(kl_ref, kc_ref, vl_ref, vc_ref, k_s, v_s)

    total = k_s.shape[1]

    @pl.when(i < nq)
    def _():
        _da_attend(q_ref, k_s, v_s, _key_blocks(0, total), lam, gain, o_ref)

    @pl.when(i >= nq)
    def _():
        _da_attend(q_ref, k_s, v_s, _key_blocks(seq, total), lam, gain, o_ref)


def _gqa_attend(q_ref, k_s, v_s, blocks, o_ref):
    tq = q_ref.shape[0]
    group_w = GQA_GROUP * HEAD_W
    queries = [jnp.concatenate([q_ref[:, h * group_w + g * HEAD_W:h * group_w + (g + 1) * HEAD_W]
                                for g in range(GQA_GROUP)], axis=0) for h in range(k_s.shape[0])]

    def finish(h, o):
        for g in range(GQA_GROUP):
            lo = h * group_w + g * HEAD_W
            o_ref[:, lo:lo + HEAD_W] = o[g * tq:(g + 1) * tq].astype(BF16)

    _softmax_pv(queries, k_s, v_s, blocks, finish)


def _gqa_attn_kernel(q_ref, kl_ref, kc_ref, vl_ref, vc_ref, o_ref, k_s, v_s, *, nq):
    i = pl.program_id(2)
    seq = kl_ref.shape[0]

    @pl.when(i == 0)
    def _():
        _stage_keys(kl_ref, kc_ref, vl_ref, vc_ref, k_s, v_s)

    total = k_s.shape[1]

    @pl.when(i < nq)
    def _():
        _gqa_attend(q_ref, k_s, v_s, _key_blocks(0, total), o_ref)

    @pl.when(i >= nq)
    def _():
        _gqa_attend(q_ref, k_s, v_s, _key_blocks(seq, total), o_ref)


def _attention(lay, kern, q, k, v, extra, extra_specs, *, kv_heads, q_per_kv, tq, need_ctx, name):
    batch, seq, ctx, n_lat = lay.batch, lay.seq, lay.ctx, lay.n_lat
    hp = ATTN_KV_PER_STEP
    assert seq % tq == 0 and ctx % tq == 0 and n_lat % ctx == 0 and kv_heads % hp == 0
    nq, nqc = seq // tq, (ctx // tq if need_ctx else 0)
    lat_rows, ctx_base = n_lat // tq, n_lat // ctx
    q_block_w, kv_block_w = hp * q_per_kv * HEAD_W, hp * HEAD_W

    def q_map(b, h, i):
        return (jnp.where(i < nq, b * nq + i, lat_rows + b * (ctx // tq) + (i - nq)), h)

    kv_lat = pl.BlockSpec((seq, kv_block_w), lambda b, h, i: (b, h))
    kv_ctx = pl.BlockSpec((ctx, kv_block_w), lambda b, h, i: (ctx_base + b, h))
    n_rows = lay.n_all if need_ctx else n_lat
    return pl.pallas_call(
        functools.partial(kern, nq=nq),
        grid=(batch, kv_heads // hp, nq + nqc),
        in_specs=[pl.BlockSpec((tq, q_block_w), q_map), kv_lat, kv_ctx, kv_lat, kv_ctx] + extra_specs,
        out_specs=pl.BlockSpec((tq, q_block_w), q_map),
        out_shape=jax.ShapeDtypeStruct((n_rows, q.shape[1]), BF16),
        scratch_shapes=[pltpu.VMEM((hp, seq + ctx, HEAD_W), BF16),
                        pltpu.VMEM((hp, seq + ctx, 2 * HEAD_W), BF16)],
        compiler_params=_params("parallel", "parallel", "arbitrary"),
        name=name,
    )(q, k, k, v, v, *extra)


def _hg_scan_kernel(q_ref, lf_ref, v_ref, o_ref, st_ref, *, reverse):
    c = HG_CHUNK
    n_sub = q_ref.shape[0] // c
    j = pl.program_id(1)

    @pl.when(j == 0)
    def _():
        st_ref[...] = jnp.zeros_like(st_ref)

    row = lax.broadcasted_iota(jnp.int32, (c, c), 0)
    col = lax.broadcasted_iota(jnp.int32, (c, c), 1)
    later, earlier = (col, row) if reverse else (row, col)
    tri = (earlier <= later).astype(F32)

    levels = []
    half = c // 2
    while half >= 1:
        same = (row // (2 * half)) == (col // (2 * half))
        q_side = (later % (2 * half)) >= half
        k_side = (earlier % (2 * half)) < half
        levels.append((half, same & q_side & k_side))
        half //= 2
    nv = c // SUBLANES
    t_loc = lax.broadcasted_iota(jnp.int32, (nv, SUBLANES, HEAD_W), 1)

    def boundary(cum, half):
        b_row = half if reverse else half - 1
        if 2 * half >= SUBLANES:
            c3 = cum.reshape(c // (2 * half), 2 * half, HEAD_W)
            return jnp.broadcast_to(c3[:, b_row:b_row + 1, :], c3.shape).reshape(c, HEAD_W)
        c3 = cum.reshape(nv, SUBLANES, HEAD_W)
        ref = None
        for blk in range(SUBLANES // (2 * half)):
            r = blk * 2 * half + b_row
            cand = jnp.broadcast_to(c3[:, r:r + 1, :], c3.shape)
            ref = cand if ref is None else jnp.where(t_loc >= blk * 2 * half, cand, ref)
        return ref.reshape(c, HEAD_W)

    heads = range(q_ref.shape[1] // HEAD_W)
    end = 0 if reverse else c - 1
    states = [st_ref[h] for h in heads]

    for sub in (reversed(range(n_sub)) if reverse else range(n_sub)):
        rows = slice(sub * c, (sub + 1) * c)
        _hg_chunk(q_ref, lf_ref, v_ref, o_ref, rows, states, tri, row == col, levels, boundary, end)

    for h in heads:
        st_ref[h] = states[h]


def _hg_chunk(q_ref, lf_ref, v_ref, o_ref, rows, states, tri, diagonal, levels, boundary, end):
    cum_all = jnp.dot(tri, lf_ref[rows, :], preferred_element_type=F32,
                      precision=lax.Precision.HIGHEST)
    staged = []
    for h in range(len(states)):
        sl = slice(h * HEAD_W, (h + 1) * HEAD_W)
        q, lf, v = q_ref[rows, sl], lf_ref[rows, sl], v_ref[rows, sl]
        cum = cum_all[:, sl]
        f = jnp.exp2(lf)
        k = 1.0 - f
        v_b, k_b = v.astype(BF16), k.astype(BF16)
        att = jnp.where(diagonal, _nt_dot(q.astype(BF16), k_b), 0.0)
        for half, mask in levels:
            if half == 1:
                qt, kt = (q * f).astype(BF16), k_b
            else:
                decay = jnp.exp2(-jnp.abs(cum - boundary(cum, half)))
                qt, kt = (q * decay).astype(BF16), (k * decay).astype(BF16)
            att = jnp.where(mask, _nt_dot(qt, kt), att)
        total = cum[end:end + 1, :]
        st = states[h]
        inter = _nt_dot((q * jnp.exp2(cum)).astype(BF16), st.astype(BF16))
        kd = (k * jnp.exp2(total - cum)).astype(BF16)
        upd = lax.dot_general(v_b, kd, (((0,), (0,)), ((), ())), preferred_element_type=F32)
        states[h] = st * jnp.exp2(total) + upd
        staged.append((sl, att.astype(BF16), v_b, inter))

    for sl, att_b, v_b, inter in staged:
        o_ref[rows, sl] = jnp.dot(att_b, v_b, preferred_element_type=F32) + inter


def _hg_scan(lay, p, d, reverse):
    batch, seq, ctx, n_lat = lay.batch, lay.seq, lay.ctx, lay.n_lat
    c = HG_CHUNK * HG_CHUNKS_PER_STEP
    assert seq % c == 0 and ctx % c == 0
    nlc, ncc = seq // c, ctx // c
    ctx_base = n_lat // c

    def blk(b, j):
        c_ctx = (ncc - 1 - j) if reverse else j
        c_lat = (nlc - 1 - (j - ncc)) if reverse else (j - ncc)
        return jnp.where(j < ncc, ctx_base + b * ncc + c_ctx, b * nlc + c_lat)

    def col_spec(colblk):
        return pl.BlockSpec((c, d), lambda b, j: (blk(b, j), colblk))

    return pl.pallas_call(
        functools.partial(_hg_scan_kernel, reverse=reverse),
        grid=(batch, ncc + nlc),
        in_specs=[col_spec(0), col_spec(2 if reverse else 1), col_spec(3)],
        out_specs=col_spec(0),
        out_shape=jax.ShapeDtypeStruct((lay.n_all, d), F32),
        scratch_shapes=[pltpu.VMEM((d // HEAD_W, HEAD_W, HEAD_W), F32)],
        compiler_params=_params("arbitrary", "arbitrary"),
        name="hg_scan_bwd" if reverse else "hg_scan_fwd",
    )(p, p, p)


def _post_mixer(x, y_b, mod_ref, g_ref, wo_ref, win_ref, wout_ref):
    y = jnp.dot(y_b, wo_ref[...], preferred_element_type=F32)
    x = x + mod_ref[5:6, :] * _rms(y, g_ref[3:4, :])
    return _ffn(x, mod_ref, g_ref, win_ref, wout_ref, 1)


def _post_kernel(x_ref, y_ref, mod_ref, g_ref, wo_ref, win_ref, wout_ref, o_ref):
    o_ref[...] = _post_mixer(x_ref[...], y_ref[...], mod_ref, g_ref, wo_ref, win_ref, wout_ref)


def _post_hg_kernel(x_ref, of_ref, ob_ref, gate_ref, hn_ref, mod_ref, g_ref, wo_ref, win_ref,
                    wout_ref, o_ref):
    o = of_ref[...] + ob_ref[...]
    gate = gate_ref[...]
    parts = []
    for h in range(o.shape[1] // HEAD_W):
        sl = slice(h * HEAD_W, (h + 1) * HEAD_W)
        parts.append((_rms(o[:, sl], hn_ref[...]) * _silu(gate[:, sl])).astype(BF16))
    o_ref[...] = _post_mixer(x_ref[...], jnp.concatenate(parts, axis=1), mod_ref, g_ref, wo_ref,
                             win_ref, wout_ref)


def _post_sublayer(lay, kern, x, ys, y_specs, extra, extra_specs, mod_all, norm_g, w_o, w_in, w_out,
                   layer, j, n_tiles, name):
    d = x.shape[1]
    return pl.pallas_call(
        kern,
        grid=(n_tiles,),
        in_specs=[lay.rows(d)] + y_specs + extra_specs
                 + [lay.mod_spec(d, layer), _resident(norm_g.shape, (layer,)),
                    _resident(w_o.shape, (j,)), _resident(w_in.shape, (layer, 1)),
                    _resident(w_out.shape, (layer, 1))],
        out_specs=lay.rows(d),
        out_shape=jax.ShapeDtypeStruct((n_tiles * lay.tm, d), F32),
        compiler_params=_params("parallel"),
        name=name,
    )(x, *ys, *extra, mod_all, norm_g, w_o, w_in, w_out)


def _rope_tables(rows, head_dim, n_id_rows, maps):
    pairs = head_dim // 4
    inv_freq = jnp.power(ROPE_THETA, -jnp.arange(pairs, dtype=F32) / pairs)
    r = jnp.repeat(jnp.arange(rows, dtype=F32), GRID_W)
    col = jnp.tile(jnp.arange(GRID_W, dtype=F32), rows)
    ang = jnp.concatenate([r[:, None] * inv_freq, col[:, None] * inv_freq], axis=-1)
    cos, sin = jnp.cos(ang), jnp.sin(ang)
    cos = jnp.tile(cos, (1, 2 * maps))
    sin = jnp.concatenate([-jnp.tile(sin, (1, maps)), jnp.tile(sin, (1, maps))], axis=-1)
    cos = jnp.concatenate([cos, jnp.ones((n_id_rows, LANES), F32)], axis=0)
    sin = jnp.concatenate([sin, jnp.zeros((n_id_rows, LANES), F32)], axis=0)
    return cos, sin


def _da_head_perm():
    p = np.arange(2)[:, None, None]
    m = np.arange(2)[None, :, None]
    j = np.arange(HEAD_W // 4)[None, None, :]
    return (m * (HEAD_W // 2) + 2 * j + p).reshape(-1)


def _gqa_head_perm():
    p = np.arange(2)[:, None]
    j = np.arange(HEAD_W // 2)[None, :]
    return (2 * j + p).reshape(-1)


def _permute_heads(n_blocks, perm, n_tail):
    idx = (np.arange(n_blocks)[:, None] * HEAD_W + perm[None, :]).reshape(-1)
    return np.concatenate([idx, n_blocks * HEAD_W + np.arange(n_tail)])


def kernel(x, c, ctx, c_ctx, w_mod, b_mod, norm_g, ffn_w_in, ffn_w_out, da_w_qkv, da_lambda, da_subln, da_w_o, hg_w_in, hg_lower_bound, hg_norm, hg_w_o, gqa_w_qkv, gqa_q_norm, gqa_k_norm, gqa_w_o):
    batch, seq, d = x.shape
    n_ctx = ctx.shape[1]
    depth = w_mod.shape[0]
    assert d == N_HEADS * HEAD_W
    lay = _Layout(batch, seq, n_ctx)
    lay_hg = _Layout(batch, seq, n_ctx, TOKEN_TILE // 2)
    rows = seq // GRID_W

    pad = (-(batch + 1)) % SUBLANES
    c_rows = jnp.concatenate([c, c_ctx[None, :], jnp.zeros((pad, d), F32)], axis=0)
    mod_all = _modulation(c_rows, w_mod, b_mod)[:, :batch + 1].reshape(depth, batch + 1, N_MOD, d)

    w_in, w_out = ffn_w_in.astype(BF16), ffn_w_out.astype(BF16)
    da_cos, da_sin = _rope_tables(rows, HEAD_W // 2, lay.tm, 2)
    gqa_cos, gqa_sin = _rope_tables(rows, HEAD_W, lay.tm, 1)
    da_w = da_w_qkv[:, :, _permute_heads(2 * N_HEADS, _da_head_perm(), d)].astype(BF16)
    gqa_perm = _gqa_head_perm()
    kvw = d // GQA_GROUP
    gqa_w = gqa_w_qkv[:, :, _permute_heads(N_HEADS + N_HEADS // GQA_GROUP, gqa_perm, kvw)].astype(BF16)
    gqa_qn, gqa_kn = gqa_q_norm[:, None, gqa_perm], gqa_k_norm[:, None, gqa_perm]
    da_wo, hg_wo, gqa_wo = da_w_o.astype(BF16), hg_w_o.astype(BF16), gqa_w_o.astype(BF16)
    hg_w = hg_w_in.astype(BF16)
    lb_table = jnp.cumsum(jax.nn.softmax(hg_lower_bound.astype(F32), axis=1), axis=1)
    lb_table = jnp.swapaxes(lb_table - lb_table[:, :1], 0, 1)
    da_gain, hg_gain = da_subln[:, None, :], hg_norm[:, None, :]

    xs = [x.reshape(batch * seq, d), ctx.reshape(batch * n_ctx, d)]
    for i in range(depth):
        kind, j = i % N_MIXERS, i // N_MIXERS
        need_ctx = i < depth - 1
        xa = _ffn_sublayer(lay, xs, mod_all, norm_g, w_in, w_out, i)
        post = functools.partial(_post_sublayer, mod_all=mod_all, norm_g=norm_g, w_in=w_in,
                                 w_out=w_out, layer=i, j=j)

        if kind == 0:
            lam_init = 0.8 - 0.6 * math.exp(-0.3 * i)
            q, k, v = _project(
                lay, functools.partial(_proj_da_kernel, q_scale=LOG2_E * (HEAD_W // 2) ** -0.5),
                xa, mod_all, norm_g, da_w, i, j, [da_cos, da_sin], [lay.rope_spec(), lay.rope_spec()],
                [d, d, d], BF16, "proj_da")
            y = _attention(
                lay, functools.partial(_da_attn_kernel, lam_init=lam_init), q, k, v,
                [da_lambda, da_gain],
                [_resident(da_lambda.shape, (j,)), _resident(da_gain.shape, (j,))],
                kv_heads=N_HEADS, q_per_kv=1, tq=DA_Q_TILE, need_ctx=need_ctx, name="attn_da")
            xa = post(lay, _post_kernel, xa, [y], [lay.rows(d)], [], [], w_o=da_wo,
                      n_tiles=lay.all_tiles if need_ctx else lay.lat_tiles, name="post_da")
        elif kind == 1:
            p, = _project(lay, _proj_hg_kernel, xa, mod_all, norm_g, hg_w, i, j,
                          [lb_table], [_resident(lb_table.shape, (i,))], [5 * d], F32, "proj_hg")
            o_f = _hg_scan(lay, p, d, reverse=False)
            o_b = _hg_scan(lay, p, d, reverse=True)
            xa = post(lay_hg, _post_hg_kernel, xa, [o_f, o_b, p],
                      [lay_hg.rows(d), lay_hg.rows(d), lay_hg.rows(d, 4)],
                      [hg_gain], [_resident(hg_gain.shape, (j,))], w_o=hg_wo,
                      n_tiles=lay_hg.all_tiles if need_ctx else lay_hg.lat_tiles, name="post_hg")
        else:
            q, k, v = _project(
                lay, functools.partial(_proj_gqa_kernel, q_scale=LOG2_E * HEAD_W ** -0.5),
                xa, mod_all, norm_g, gqa_w, i, j,
                [gqa_cos, gqa_sin, gqa_qn, gqa_kn],
                [lay.rope_spec(), lay.rope_spec(), _resident(gqa_qn.shape, (j,)),
                 _resident(gqa_kn.shape, (j,))],
                [d, kvw, kvw], BF16, "proj_gqa")
            y = _attention(lay, _gqa_attn_kernel, q, k, v, [], [],
                           kv_heads=N_HEADS // GQA_GROUP, q_per_kv=GQA_GROUP,
                           tq=GQA_Q_TILE, need_ctx=need_ctx, name="attn_gqa")
            xa = post(lay, _post_kernel, xa, [y], [lay.rows(d)], [], [], w_o=gqa_wo,
                      n_tiles=lay.all_tiles if need_ctx else lay.lat_tiles, name="post_gqa")
        xs = [xa]
    return xa.reshape(batch, seq, d)
```

```python
import functools
import math

import numpy as np
import jax
import jax.numpy as jnp
from jax import lax
from jax.experimental import pallas as pl
from jax.experimental.pallas import tpu as pltpu

F32 = jnp.float32
BF16 = jnp.bfloat16

NORM_EPS = 1e-6
ROPE_THETA = 10000.0
GRID_W = 64
N_MOD = 9
N_MIXERS = 3
MACARON_WEIGHT = 0.5
LOG2_E = math.log2(math.e)
LANES = 128
SUBLANES = 8
MXU_W = 256
HEAD_W = 128
N_HEADS = 8
GQA_GROUP = 4
VMEM_LIMIT = 56 * 1024 * 1024

TOKEN_TILE = 512
DA_Q_TILE = 256
GQA_Q_TILE = 128
KEY_BLOCK = 512
ATTN_KV_PER_STEP = 2
SCORE_LOOKAHEAD = 1
PROJ_CHUNK = 512
HG_CHUNK = 64
HG_CHUNKS_PER_STEP = 4


def _params(*sem):
    return pltpu.CompilerParams(dimension_semantics=sem, vmem_limit_bytes=VMEM_LIMIT)


def _resident(shape, lead=()):
    block = (None,) * len(lead) + tuple(shape[len(lead):])
    index = tuple(lead) + (0,) * (len(shape) - len(lead))
    return pl.BlockSpec(block, lambda *_: index, pipeline_mode=pl.Buffered(1))


def _rms(x, g):
    return x * lax.rsqrt(jnp.mean(x * x, axis=-1, keepdims=True) + NORM_EPS) * g


def _modulated(x, g, shift, scale):
    return _rms(x, g * (1.0 + scale)) + shift


def _silu(x):
    return x * jax.nn.sigmoid(x)


def _mod_kernel(c_ref, w_ref, b_ref, o_ref):
    s = _silu(c_ref[...])
    o_ref[0] = jnp.dot(s, w_ref[0], preferred_element_type=F32,
                       precision=lax.Precision.HIGHEST) + b_ref[0]


def _modulation(c_rows, w_mod, b_mod):
    depth, d, n = w_mod.shape
    rows = c_rows.shape[0]
    tn = n // 8
    return pl.pallas_call(
        _mod_kernel,
        grid=(depth, n // tn),
        in_specs=[pl.BlockSpec((rows, d), lambda i, j: (0, 0)),
                  pl.BlockSpec((1, d, tn), lambda i, j: (i, 0, j)),
                  pl.BlockSpec((1, 1, tn), lambda i, j: (i, 0, j))],
        out_specs=pl.BlockSpec((1, rows, tn), lambda i, j: (i, 0, j)),
        out_shape=jax.ShapeDtypeStruct((depth, rows, n), F32),
        compiler_params=_params("arbitrary", "arbitrary"),
        name="modulation",
    )(c_rows, w_mod, b_mod.reshape(depth, 1, n))


class _Layout:
    def __init__(self, batch, seq, ctx, tm=TOKEN_TILE):
        self.batch, self.seq, self.ctx = batch, seq, ctx
        self.n_lat = batch * seq
        self.n_all = self.n_lat + batch * ctx
        assert seq % tm == 0 and (batch * ctx) % tm == 0
        self.tm = tm
        self.lat_tiles = self.n_lat // tm
        self.all_tiles = self.n_all // tm
        self.tiles_per_sample = seq // tm

    def mod_spec(self, d, layer):
        lat_tiles, tps, batch = self.lat_tiles, self.tiles_per_sample, self.batch
        return pl.BlockSpec((None, None, N_MOD, d),
                            lambda t: (layer, jnp.where(t < lat_tiles, t // tps, batch), 0, 0))

    def rope_spec(self):
        lat_tiles, tps = self.lat_tiles, self.tiles_per_sample
        return pl.BlockSpec((self.tm, LANES),
                            lambda t: (jnp.where(t < lat_tiles, t % tps, tps), 0))

    def rows(self, width, col=0):
        return pl.BlockSpec((self.tm, width), lambda t: (t, col))


def _ffn_chunks(d_ff):
    assert d_ff % MXU_W == 0
    first = (d_ff // MXU_W + 1) // 2 * MXU_W
    return [(0, first), (first, d_ff - first)]


def _ffn(x, mod_ref, g_ref, win_ref, wout_ref, which):
    mod_base, g_base = 6 * which, 4 * which
    d_ff = wout_ref.shape[0]
    shift = mod_ref[mod_base:mod_base + 1, :]
    scale = mod_ref[mod_base + 1:mod_base + 2, :]
    gate = mod_ref[mod_base + 2:mod_base + 3, :]
    h = _modulated(x, g_ref[g_base:g_base + 1, :], shift, scale).astype(BF16)
    chunks = _ffn_chunks(d_ff)
    hidden = [(jnp.dot(h, win_ref[:, lo:lo + n], preferred_element_type=F32),
               jnp.dot(h, win_ref[:, d_ff + lo:d_ff + lo + n], preferred_element_type=F32))
              for lo, n in chunks]
    y = None
    for (lo, n), (gt, up) in zip(chunks, hidden):
        a = (_silu(gt) * up).astype(BF16)
        part = jnp.dot(a, wout_ref[lo:lo + n, :], preferred_element_type=F32)
        y = part if y is None else y + part
    return x + _rms(y, (MACARON_WEIGHT * gate) * g_ref[g_base + 1:g_base + 2, :])


def _ffn_kernel(x_ref, mod_ref, g_ref, win_ref, wout_ref, o_ref):
    o_ref[...] = _ffn(x_ref[...], mod_ref, g_ref, win_ref, wout_ref, 0)


def _ffn_first_kernel(xl_ref, xc_ref, mod_ref, g_ref, win_ref, wout_ref, o_ref, *, lat_tiles):
    x = jnp.where(pl.program_id(0) < lat_tiles, xl_ref[...], xc_ref[...])
    o_ref[...] = _ffn(x, mod_ref, g_ref, win_ref, wout_ref, 0)


def _ffn_sublayer(lay, xs, mod_all, norm_g, w_in, w_out, layer):
    d = xs[0].shape[1]
    if len(xs) == 1:
        kern, x_specs = _ffn_kernel, [lay.rows(d)]
    else:
        lat_tiles, tm = lay.lat_tiles, lay.tm
        kern = functools.partial(_ffn_first_kernel, lat_tiles=lat_tiles)
        x_specs = [pl.BlockSpec((tm, d), lambda t: (jnp.minimum(t, lat_tiles - 1), 0)),
                   pl.BlockSpec((tm, d), lambda t: (jnp.maximum(t - lat_tiles, 0), 0))]
    return pl.pallas_call(
        kern,
        grid=(lay.all_tiles,),
        in_specs=x_specs + [lay.mod_spec(d, layer), _resident(norm_g.shape, (layer,)),
                            _resident(w_in.shape, (layer, 0)), _resident(w_out.shape, (layer, 0))],
        out_specs=lay.rows(d),
        out_shape=jax.ShapeDtypeStruct((lay.n_all, d), F32),
        compiler_params=_params("parallel"),
        name="ffn0",
    )(*xs, mod_all, norm_g, w_in, w_out)


def _mixer_matmul(x_ref, mod_ref, g_ref, w_ref):
    h = _modulated(x_ref[...], g_ref[2:3, :], mod_ref[3:4, :], mod_ref[4:5, :]).astype(BF16)
    width = w_ref.shape[1]
    assert width % PROJ_CHUNK == 0
    parts = [jnp.dot(h, w_ref[:, lo:lo + PROJ_CHUNK], preferred_element_type=F32)
             for lo in range(0, width, PROJ_CHUNK)]

    def cols(lo, n):
        part, off = parts[lo // PROJ_CHUNK], lo % PROJ_CHUNK
        assert off + n <= PROJ_CHUNK
        return part[:, off:off + n]
    return cols


def _rope(x, cos, sin):
    return x * cos + pltpu.roll(x, LANES // 2, 1) * sin


def _proj_da_kernel(x_ref, mod_ref, g_ref, w_ref, cos_ref, sin_ref, q_ref, k_ref, v_ref, *, q_scale):
    qkv = _mixer_matmul(x_ref, mod_ref, g_ref, w_ref)
    cos, sin = cos_ref[...], sin_ref[...]
    width = q_ref.shape[1]
    cos_q, sin_q = cos * q_scale, sin * q_scale
    for lo in range(0, width, HEAD_W):
        q_ref[:, lo:lo + HEAD_W] = _rope(qkv(lo, HEAD_W), cos_q, sin_q).astype(BF16)
    for lo in range(0, width, HEAD_W):
        k_ref[:, lo:lo + HEAD_W] = _rope(qkv(width + lo, HEAD_W), cos, sin).astype(BF16)
    for lo in range(0, width, HEAD_W):
        v_ref[:, lo:lo + HEAD_W] = qkv(2 * width + lo, HEAD_W).astype(BF16)


def _proj_gqa_kernel(x_ref, mod_ref, g_ref, w_ref, cos_ref, sin_ref, qn_ref, kn_ref,
                     q_ref, k_ref, v_ref, *, q_scale):
    qkv = _mixer_matmul(x_ref, mod_ref, g_ref, w_ref)
    cos, sin = cos_ref[...], sin_ref[...]
    qw, kw = q_ref.shape[1], k_ref.shape[1]
    for lo in range(0, qw, HEAD_W):
        xq = _rms(qkv(lo, HEAD_W), qn_ref[...] * q_scale)
        q_ref[:, lo:lo + HEAD_W] = _rope(xq, cos, sin).astype(BF16)
    for lo in range(0, kw, HEAD_W):
        xk = _rms(qkv(qw + lo, HEAD_W), kn_ref[...])
        k_ref[:, lo:lo + HEAD_W] = _rope(xk, cos, sin).astype(BF16)
    for lo in range(0, v_ref.shape[1], HEAD_W):
        v_ref[:, lo:lo + HEAD_W] = qkv(qw + kw + lo, HEAD_W).astype(BF16)


def _proj_hg_kernel(x_ref, mod_ref, g_ref, w_ref, lb_ref, o_ref):
    p = _mixer_matmul(x_ref, mod_ref, g_ref, w_ref)
    d = x_ref.shape[1]
    n = PROJ_CHUNK
    for lo in range(0, d, n):
        o_ref[:, lo:lo + n] = _silu(p(lo, n))
    for k in range(2):
        for lo in range(0, d, n):
            lb = lb_ref[k:k + 1, lo:lo + n]
            f = lb + (1.0 - lb) * jax.nn.sigmoid(p((1 + k) * d + lo, n))
            o_ref[:, (1 + k) * d + lo:(1 + k) * d + lo + n] = jnp.log2(f)
    for lo in range(3 * d, 5 * d, n):
        o_ref[:, lo:lo + n] = p(lo, n)


def _project(lay, kern, x, mod_all, norm_g, w, layer, j, extra, extra_specs, out_widths, out_dtype,
             name):
    d = x.shape[1]
    outs = pl.pallas_call(
        kern,
        grid=(lay.all_tiles,),
        in_specs=[lay.rows(d), lay.mod_spec(d, layer), _resident(norm_g.shape, (layer,)),
                  _resident(w.shape, (j,))] + extra_specs,
        out_specs=[lay.rows(wd) for wd in out_widths],
        out_shape=[jax.ShapeDtypeStruct((lay.n_all, wd), out_dtype) for wd in out_widths],
        compiler_params=_params("parallel"),
        name=name,
    )(x, mod_all, norm_g, w, *extra)
    return outs


def _nt_dot(a, b):
    return lax.dot_general(a, b, (((1,), (1,)), ((), ())), preferred_element_type=F32)


def _stage_keys(kl_ref, kc_ref, vl_ref, vc_ref, k_s, v_s):
    seq = kl_ref.shape[0]
    for h in range(k_s.shape[0]):
        sl = slice(h * HEAD_W, (h + 1) * HEAD_W)
        k_s[h, 0:seq, :] = kl_ref[:, sl]
        k_s[h, seq:, :] = kc_ref[:, sl]
        v_s[h, 0:seq, 0:HEAD_W] = vl_ref[:, sl]
        v_s[h, seq:, 0:HEAD_W] = vc_ref[:, sl]
        v_s[h, :, HEAD_W:] = jnp.ones((v_s.shape[1], HEAD_W), BF16)


def _key_blocks(start, stop):
    return [(s, min(KEY_BLOCK, stop - s)) for s in range(start, stop, KEY_BLOCK)]


def _softmax_pv(queries, k_s, v_s, blocks, finish):
    steps = [(h, blk) for h in range(len(queries)) for blk in blocks]

    def scores(step):
        h, (start, size) = step
        return _nt_dot(queries[h], k_s[h, start:start + size, :])

    m = acc = None
    ahead = [scores(step) for step in steps[:SCORE_LOOKAHEAD]]
    for n, (h, (start, size)) in enumerate(steps):
        s = ahead.pop(0)
        if n + SCORE_LOOKAHEAD < len(steps):
            ahead.append(scores(steps[n + SCORE_LOOKAHEAD]))
        m_new = jnp.max(s, axis=-1, keepdims=True)
        if m is not None:
            m_new = jnp.maximum(m, m_new)
        e = jnp.exp2(s - m_new).astype(BF16)
        pv = jnp.dot(e, v_s[h, start:start + size, :], preferred_element_type=F32)
        acc = pv if acc is None else acc * jnp.exp2(m - m_new) + pv
        m = m_new
        if (start, size) == blocks[-1]:
            finish(h, acc[:, :HEAD_W] / acc[:, HEAD_W:])
            m = acc = None


def _da_attend(q_ref, k_s, v_s, blocks, lam, gain, o_ref):
    tq = q_ref.shape[0]
    lane = lax.broadcasted_iota(jnp.int32, (1, LANES), 1)
    first_map = (lane % (LANES // 2)) < (LANES // 4)
    queries = []
    for h in range(k_s.shape[0]):
        q = q_ref[:, h * HEAD_W:(h + 1) * HEAD_W]
        zero = jnp.zeros_like(q)
        queries.append(jnp.concatenate([jnp.where(first_map, q, zero),
                                        jnp.where(first_map, zero, q)], axis=0))

    def finish(h, p):
        o_ref[:, h * HEAD_W:(h + 1) * HEAD_W] = _rms(p[:tq] - lam * p[tq:], gain).astype(BF16)

    _softmax_pv(queries, k_s, v_s, blocks, finish)


def _da_attn_kernel(q_ref, kl_ref, kc_ref, vl_ref, vc_ref, lam_ref, g_ref, o_ref, k_s, v_s,
                    *, nq, lam_init):
    lp = lam_ref[...]
    lam = (jnp.exp(jnp.sum(lp[0:1] * lp[1:2], axis=-1, keepdims=True))
           - jnp.exp(jnp.sum(lp[2:3] * lp[3:4], axis=-1, keepdims=True)) + lam_init)
    gain = g_ref[...] * (1.0 - lam_init)
    i = pl.program_id(2)
    seq = kl_ref.shape[0]

    @pl.when(i == 0)
    def _():
        _stage_keys(kl_ref, kc_ref, vl_ref, vc_ref, k_s, v_s)

    total = k_s.shape[1]

    @pl.when(i < nq)
    def _():
        _da_attend(q_ref, k_s, v_s, _key_blocks(0, total), lam, gain, o_ref)

    @pl.when(i >= nq)
    def _():
        _da_attend(q_ref, k_s, v_s, _key_blocks(seq, total), lam, gain, o_ref)


def _gqa_attend(q_ref, k_s, v_s, blocks, o_ref):
    tq = q_ref.shape[0]
    group_w = GQA_GROUP * HEAD_W
    queries = [jnp.concatenate([q_ref[:, h * group_w + g * HEAD_W:h * group_w + (g + 1) * HEAD_W]
                                for g in range(GQA_GROUP)], axis=0) for h in range(k_s.shape[0])]

    def finish(h, o):
        for g in range(GQA_GROUP):
            lo = h * group_w + g * HEAD_W
            o_ref[:, lo:lo + HEAD_W] = o[g * tq:(g + 1) * tq].astype(BF16)

    _softmax_pv(queries, k_s, v_s, blocks, finish)


def _gqa_attn_kernel(q_ref, kl_ref, kc_ref, vl_ref, vc_ref, o_ref, k_s, v_s, *, nq):
    i = pl.program_id(2)
    seq = kl_ref.shape[0]

    @pl.when(i == 0)
    def _():
        _stage_keys(kl_ref, kc_ref, vl_ref, vc_ref, k_s, v_s)

    total = k_s.shape[1]

    @pl.when(i < nq)
    def _():
        _gqa_attend(q_ref, k_s, v_s, _key_blocks(0, total), o_ref)

    @pl.when(i >= nq)
    def _():
        _gqa_attend(q_ref, k_s, v_s, _key_blocks(seq, total), o_ref)


def _attention(lay, kern, q, k, v, extra, extra_specs, *, kv_heads, q_per_kv, tq, need_ctx, name):
    batch, seq, ctx, n_lat = lay.batch, lay.seq, lay.ctx, lay.n_lat
    hp = ATTN_KV_PER_STEP
    assert seq % tq == 0 and ctx % tq == 0 and n_lat % ctx == 0 and kv_heads % hp == 0
    nq, nqc = seq // tq, (ctx // tq if need_ctx else 0)
    lat_rows, ctx_base = n_lat // tq, n_lat // ctx
    q_block_w, kv_block_w = hp * q_per_kv * HEAD_W, hp * HEAD_W

    def q_map(b, h, i):
        return (jnp.where(i < nq, b * nq + i, lat_rows + b * (ctx // tq) + (i - nq)), h)

    kv_lat = pl.BlockSpec((seq, kv_block_w), lambda b, h, i: (b, h))
    kv_ctx = pl.BlockSpec((ctx, kv_block_w), lambda b, h, i: (ctx_base + b, h))
    n_rows = lay.n_all if need_ctx else n_lat
    return pl.pallas_call(
        functools.partial(kern, nq=nq),
        grid=(batch, kv_heads // hp, nq + nqc),
        in_specs=[pl.BlockSpec((tq, q_block_w), q_map), kv_lat, kv_ctx, kv_lat, kv_ctx] + extra_specs,
        out_specs=pl.BlockSpec((tq, q_block_w), q_map),
        out_shape=jax.ShapeDtypeStruct((n_rows, q.shape[1]), BF16),
        scratch_shapes=[pltpu.VMEM((hp, seq + ctx, HEAD_W), BF16),
                        pltpu.VMEM((hp, seq + ctx, 2 * HEAD_W), BF16)],
        compiler_params=_params("parallel", "parallel", "arbitrary"),
        name=name,
    )(q, k, k, v, v, *extra)


def _hg_scan_kernel(q_ref, lf_ref, v_ref, o_ref, st_ref, *, reverse):
    c = HG_CHUNK
    n_sub = q_ref.shape[0] // c
    j = pl.program_id(1)

    @pl.when(j == 0)
    def _():
        st_ref[...] = jnp.zeros_like(st_ref)

    row = lax.broadcasted_iota(jnp.int32, (c, c), 0)
    col = lax.broadcasted_iota(jnp.int32, (c, c), 1)
    later, earlier = (col, row) if reverse else (row, col)
    tri = (earlier <= later).astype(F32)

    levels = []
    half = c // 2
    while half >= 1:
        same = (row // (2 * half)) == (col // (2 * half))
        q_side = (later % (2 * half)) >= half
        k_side = (earlier % (2 * half)) < half
        levels.append((half, same & q_side & k_side))
        half //= 2
    nv = c // SUBLANES
    t_loc = lax.broadcasted_iota(jnp.int32, (nv, SUBLANES, HEAD_W), 1)

    def boundary(cum, half):
        b_row = half if reverse else half - 1
        if 2 * half >= SUBLANES:
            c3 = cum.reshape(c // (2 * half), 2 * half, HEAD_W)
            return jnp.broadcast_to(c3[:, b_row:b_row + 1, :], c3.shape).reshape(c, HEAD_W)
        c3 = cum.reshape(nv, SUBLANES, HEAD_W)
        ref = None
        for blk in range(SUBLANES // (2 * half)):
            r = blk * 2 * half + b_row
            cand = jnp.broadcast_to(c3[:, r:r + 1, :], c3.shape)
            ref = cand if ref is None else jnp.where(t_loc >= blk * 2 * half, cand, ref)
        return ref.reshape(c, HEAD_W)

    heads = range(q_ref.shape[1] // HEAD_W)
    end = 0 if reverse else c - 1
    states = [st_ref[h] for h in heads]

    for sub in (reversed(range(n_sub)) if reverse else range(n_sub)):
        rows = slice(sub * c, (sub + 1) * c)
        _hg_chunk(q_ref, lf_ref, v_ref, o_ref, rows, states, tri, row == col, levels, boundary, end)

    for h in heads:
        st_ref[h] = states[h]


def _hg_chunk(q_ref, lf_ref, v_ref, o_ref, rows, states, tri, diagonal, levels, boundary, end):
    cum_all = jnp.dot(tri, lf_ref[rows, :], preferred_element_type=F32,
                      precision=lax.Precision.HIGHEST)
    staged = []
    for h in range(len(states)):
        sl = slice(h * HEAD_W, (h + 1) * HEAD_W)
        q, lf, v = q_ref[rows, sl], lf_ref[rows, sl], v_ref[rows, sl]
        cum = cum_all[:, sl]
        f = jnp.exp2(lf)
        k = 1.0 - f
        v_b, k_b = v.astype(BF16), k.astype(BF16)
        att = jnp.where(diagonal, _nt_dot(q.astype(BF16), k_b), 0.0)
        for half, mask in levels:
            if half == 1:
                qt, kt = (q * f).astype(BF16), k_b
            else:
                decay = jnp.exp2(-jnp.abs(cum - boundary(cum, half)))
                qt, kt = (q * decay).astype(BF16), (k * decay).astype(BF16)
            att = jnp.where(mask, _nt_dot(qt, kt), att)
        total = cum[end:end + 1, :]
        st = states[h]
        inter = _nt_dot((q * jnp.exp2(cum)).astype(BF16), st.astype(BF16))
        kd = (k * jnp.exp2(total - cum)).astype(BF16)
        upd = lax.dot_general(v_b, kd, (((0,), (0,)), ((), ())), preferred_element_type=F32)
        states[h] = st * jnp.exp2(total) + upd
        staged.append((sl, att.astype(BF16), v_b, inter))

    for sl, att_b, v_b, inter in staged:
        o_ref[rows, sl] = jnp.dot(att_b, v_b, preferred_element_type=F32) + inter


def _hg_scan(lay, p, d, reverse):
    batch, seq, ctx, n_lat = lay.batch, lay.seq, lay.ctx, lay.n_lat
    c = HG_CHUNK * HG_CHUNKS_PER_STEP
    assert seq % c == 0 and ctx % c == 0
    nlc, ncc = seq // c, ctx // c
    ctx_base = n_lat // c

    def blk(b, j):
        c_ctx = (ncc - 1 - j) if reverse else j
        c_lat = (nlc - 1 - (j - ncc)) if reverse else (j - ncc)
        return jnp.where(j < ncc, ctx_base + b * ncc + c_ctx, b * nlc + c_lat)

    def col_spec(colblk):
        return pl.BlockSpec((c, d), lambda b, j: (blk(b, j), colblk))

    return pl.pallas_call(
        functools.partial(_hg_scan_kernel, reverse=reverse),
        grid=(batch, ncc + nlc),
        in_specs=[col_spec(0), col_spec(2 if reverse else 1), col_spec(3)],
        out_specs=col_spec(0),
        out_shape=jax.ShapeDtypeStruct((lay.n_all, d), F32),
        scratch_shapes=[pltpu.VMEM((d // HEAD_W, HEAD_W, HEAD_W), F32)],
        compiler_params=_params("arbitrary", "arbitrary"),
        name="hg_scan_bwd" if reverse else "hg_scan_fwd",
    )(p, p, p)


def _post_mixer(x, y_b, mod_ref, g_ref, wo_ref, win_ref, wout_ref):
    y = jnp.dot(y_b, wo_ref[...], preferred_element_type=F32)
    x = x + _rms(y, mod_ref[5:6, :] * g_ref[3:4, :])
    return _ffn(x, mod_ref, g_ref, win_ref, wout_ref, 1)


def _post_kernel(x_ref, y_ref, mod_ref, g_ref, wo_ref, win_ref, wout_ref, o_ref):
    o_ref[...] = _post_mixer(x_ref[...], y_ref[...], mod_ref, g_ref, wo_ref, win_ref, wout_ref)


def _post_hg_kernel(x_ref, of_ref, ob_ref, gate_ref, hn_ref, mod_ref, g_ref, wo_ref, win_ref,
                    wout_ref, o_ref):
    o = of_ref[...] + ob_ref[...]
    gate = gate_ref[...]
    parts = []
    for h in range(o.shape[1] // HEAD_W):
        sl = slice(h * HEAD_W, (h + 1) * HEAD_W)
        parts.append((_rms(o[:, sl], hn_ref[...]) * _silu(gate[:, sl])).astype(BF16))
    o_ref[...] = _post_mixer(x_ref[...], jnp.concatenate(parts, axis=1), mod_ref, g_ref, wo_ref,
                             win_ref, wout_ref)


def _post_sublayer(lay, kern, x, ys, y_specs, extra, extra_specs, mod_all, norm_g, w_o, w_in, w_out,
                   layer, j, n_tiles, name):
    d = x.shape[1]
    return pl.pallas_call(
        kern,
        grid=(n_tiles,),
        in_specs=[lay.rows(d)] + y_specs + extra_specs
                 + [lay.mod_spec(d, layer), _resident(norm_g.shape, (layer,)),
                    _resident(w_o.shape, (j,)), _resident(w_in.shape, (layer, 1)),
                    _resident(w_out.shape, (layer, 1))],
        out_specs=lay.rows(d),
        out_shape=jax.ShapeDtypeStruct((n_tiles * lay.tm, d), F32),
        compiler_params=_params("parallel"),
        name=name,
    )(x, *ys, *extra, mod_all, norm_g, w_o, w_in, w_out)


def _rope_tables(rows, head_dim, n_id_rows, maps):
    pairs = head_dim // 4
    inv_freq = jnp.power(ROPE_THETA, -jnp.arange(pairs, dtype=F32) / pairs)
    r = jnp.repeat(jnp.arange(rows, dtype=F32), GRID_W)
    col = jnp.tile(jnp.arange(GRID_W, dtype=F32), rows)
    ang = jnp.concatenate([r[:, None] * inv_freq, col[:, None] * inv_freq], axis=-1)
    cos, sin = jnp.cos(ang), jnp.sin(ang)
    cos = jnp.tile(cos, (1, 2 * maps))
    sin = jnp.concatenate([-jnp.tile(sin, (1, maps)), jnp.tile(sin, (1, maps))], axis=-1)
    cos = jnp.concatenate([cos, jnp.ones((n_id_rows, LANES), F32)], axis=0)
    sin = jnp.concatenate([sin, jnp.zeros((n_id_rows, LANES), F32)], axis=0)
    return cos, sin


def _da_head_perm():
    p = np.arange(2)[:, None, None]
    m = np.arange(2)[None, :, None]
    j = np.arange(HEAD_W // 4)[None, None, :]
    return (m * (HEAD_W // 2) + 2 * j + p).reshape(-1)


def _gqa_head_perm():
    p = np.arange(2)[:, None]
    j = np.arange(HEAD_W // 2)[None, :]
    return (2 * j + p).reshape(-1)


def _permute_heads(n_blocks, perm, n_tail):
    idx = (np.arange(n_blocks)[:, None] * HEAD_W + perm[None, :]).reshape(-1)
    return np.concatenate([idx, n_blocks * HEAD_W + np.arange(n_tail)])


def kernel(x, c, ctx, c_ctx, w_mod, b_mod, norm_g, ffn_w_in, ffn_w_out, da_w_qkv, da_lambda, da_subln, da_w_o, hg_w_in, hg_lower_bound, hg_norm, hg_w_o, gqa_w_qkv, gqa_q_norm, gqa_k_norm, gqa_w_o):
    batch, seq, d = x.shape
    n_ctx = ctx.shape[1]
    depth = w_mod.shape[0]
    assert d == N_HEADS * HEAD_W
    lay = _Layout(batch, seq, n_ctx)
    lay_hg = _Layout(batch, seq, n_ctx, TOKEN_TILE // 2)
    rows = seq // GRID_W

    pad = (-(batch + 1)) % SUBLANES
    c_rows = jnp.concatenate([c, c_ctx[None, :], jnp.zeros((pad, d), F32)], axis=0)
    mod_all = _modulation(c_rows, w_mod, b_mod)[:, :batch + 1].reshape(depth, batch + 1, N_MOD, d)

    w_in, w_out = ffn_w_in.astype(BF16), ffn_w_out.astype(BF16)
    da_cos, da_sin = _rope_tables(rows, HEAD_W // 2, lay.tm, 2)
    gqa_cos, gqa_sin = _rope_tables(rows, HEAD_W, lay.tm, 1)
    da_w = da_w_qkv[:, :, _permute_heads(2 * N_HEADS, _da_head_perm(), d)].astype(BF16)
    gqa_perm = _gqa_head_perm()
    kvw = d // GQA_GROUP
    gqa_w = gqa_w_qkv[:, :, _permute_heads(N_HEADS + N_HEADS // GQA_GROUP, gqa_perm, kvw)].astype(BF16)
    gqa_qn, gqa_kn = gqa_q_norm[:, None, gqa_perm], gqa_k_norm[:, None, gqa_perm]
    da_wo, hg_wo, gqa_wo = da_w_o.astype(BF16), hg_w_o.astype(BF16), gqa_w_o.astype(BF16)
    hg_w = hg_w_in.astype(BF16)
    lb_table = jnp.cumsum(jax.nn.softmax(hg_lower_bound.astype(F32), axis=1), axis=1)
    lb_table = jnp.swapaxes(lb_table - lb_table[:, :1], 0, 1)
    da_gain, hg_gain = da_subln[:, None, :], hg_norm[:, None, :]

    xs = [x.reshape(batch * seq, d), ctx.reshape(batch * n_ctx, d)]
    for i in range(depth):
        kind, j = i % N_MIXERS, i // N_MIXERS
        need_ctx = i < depth - 1
        xa = _ffn_sublayer(lay, xs, mod_all, norm_g, w_in, w_out, i)
        post = functools.partial(_post_sublayer, mod_all=mod_all, norm_g=norm_g, w_in=w_in,
                                 w_out=w_out, layer=i, j=j)

        if kind == 0:
            lam_init = 0.8 - 0.6 * math.exp(-0.3 * i)
            q, k, v = _project(
                lay, functools.partial(_proj_da_kernel, q_scale=LOG2_E * (HEAD_W // 2) ** -0.5),
                xa, mod_all, norm_g, da_w, i, j, [da_cos, da_sin], [lay.rope_spec(), lay.rope_spec()],
                [d, d, d], BF16, "proj_da")
            y = _attention(
                lay, functools.partial(_da_attn_kernel, lam_init=lam_init), q, k, v,
                [da_lambda, da_gain],
                [_resident(da_lambda.shape, (j,)), _resident(da_gain.shape, (j,))],
                kv_heads=N_HEADS, q_per_kv=1, tq=DA_Q_TILE, need_ctx=need_ctx, name="attn_da")
            xa = post(lay, _post_kernel, xa, [y], [lay.rows(d)], [], [], w_o=da_wo,
                      n_tiles=lay.all_tiles if need_ctx else lay.lat_tiles, name="post_da")
        elif kind == 1:
            p, = _project(lay, _proj_hg_kernel, xa, mod_all, norm_g, hg_w, i, j,
                          [lb_table], [_resident(lb_table.shape, (i,))], [5 * d], F32, "proj_hg")
            o_f = _hg_scan(lay, p, d, reverse=False)
            o_b = _hg_scan(lay, p, d, reverse=True)
            xa = post(lay_hg, _post_hg_kernel, xa, [o_f, o_b, p],
                      [lay_hg.rows(d), lay_hg.rows(d), lay_hg.rows(d, 4)],
                      [hg_gain], [_resident(hg_gain.shape, (j,))], w_o=hg_wo,
                      n_tiles=lay_hg.all_tiles if need_ctx else lay_hg.lat_tiles, name="post_hg")
        else:
            q, k, v = _project(
                lay, functools.partial(_proj_gqa_kernel, q_scale=LOG2_E * HEAD_W ** -0.5),
                xa, mod_all, norm_g, gqa_w, i, j,
                [gqa_cos, gqa_sin, gqa_qn, gqa_kn],
                [lay.rope_spec(), lay.rope_spec(), _resident(gqa_qn.shape, (j,)),
                 _resident(gqa_kn.shape, (j,))],
                [d, kvw, kvw], BF16, "proj_gqa")
            y = _attention(lay, _gqa_attn_kernel, q, k, v, [], [],
                           kv_heads=N_HEADS // GQA_GROUP, q_per_kv=GQA_GROUP,
                           tq=GQA_Q_TILE, need_ctx=need_ctx, name="attn_gqa")
            xa = post(lay, _post_kernel, xa, [y], [lay.rows(d)], [], [], w_o=gqa_wo,
                      n_tiles=lay.all_tiles if need_ctx else lay.lat_tiles, name="post_gqa")
        xs = [xa]
    return xa.reshape(batch, seq, d)
```

```python
import functools
import math

import numpy as np
import jax
import jax.numpy as jnp
from jax import lax
from jax.experimental import pallas as pl
from jax.experimental.pallas import tpu as pltpu

F32 = jnp.float32
BF16 = jnp.bfloat16

NORM_EPS = 1e-6
ROPE_THETA = 10000.0
GRID_W = 64
N_MOD = 9
N_MIXERS = 3
MACARON_WEIGHT = 0.5
LOG2_E = math.log2(math.e)
LANES = 128
SUBLANES = 8
MXU_W = 256
HEAD_W = 128
N_HEADS = 8
GQA_GROUP = 4
VMEM_LIMIT = 56 * 1024 * 1024

TOKEN_TILE = 1024
PROJ_TILE = 512
DA_Q_TILE = 256
GQA_Q_TILE = 128
KEY_BLOCK = 512
ATTN_KV_PER_STEP = 2
SCORE_LOOKAHEAD = 1
HG_CHUNK = 64
HG_CHUNKS_PER_STEP = 4


def _params(*sem):
    return pltpu.CompilerParams(dimension_semantics=sem, vmem_limit_bytes=VMEM_LIMIT)


def _resident(shape, lead=()):
    block = (None,) * len(lead) + tuple(shape[len(lead):])
    index = tuple(lead) + (0,) * (len(shape) - len(lead))
    return pl.BlockSpec(block, lambda *_: index, pipeline_mode=pl.Buffered(1))


def _rms(x, g):
    return x * lax.rsqrt(jnp.mean(x * x, axis=-1, keepdims=True) + NORM_EPS) * g


def _modulated(x, g, shift, scale):
    return _rms(x, g * (1.0 + scale)) + shift


def _silu(x):
    return x * jax.nn.sigmoid(x)


def _mod_kernel(c_ref, w_ref, b_ref, o_ref):
    s = _silu(c_ref[...])
    o_ref[0] = jnp.dot(s, w_ref[0], preferred_element_type=F32,
                       precision=lax.Precision.HIGHEST) + b_ref[0]


def _modulation(c_rows, w_mod, b_mod):
    depth, d, n = w_mod.shape
    rows = c_rows.shape[0]
    tn = n // 8
    return pl.pallas_call(
        _mod_kernel,
        grid=(depth, n // tn),
        in_specs=[pl.BlockSpec((rows, d), lambda i, j: (0, 0)),
                  pl.BlockSpec((1, d, tn), lambda i, j: (i, 0, j)),
                  pl.BlockSpec((1, 1, tn), lambda i, j: (i, 0, j))],
        out_specs=pl.BlockSpec((1, rows, tn), lambda i, j: (i, 0, j)),
        out_shape=jax.ShapeDtypeStruct((depth, rows, n), F32),
        compiler_params=_params("arbitrary", "arbitrary"),
        name="modulation",
    )(c_rows, w_mod, b_mod.reshape(depth, 1, n))


class _Layout:
    def __init__(self, batch, seq, ctx, tm=TOKEN_TILE):
        self.batch, self.seq, self.ctx = batch, seq, ctx
        self.n_lat = batch * seq
        self.n_all = self.n_lat + batch * ctx
        assert seq % tm == 0 and (batch * ctx) % tm == 0
        self.tm = tm
        self.lat_tiles = self.n_lat // tm
        self.all_tiles = self.n_all // tm
        self.tiles_per_sample = seq // tm

    def mod_spec(self, d, layer):
        lat_tiles, tps, batch = self.lat_tiles, self.tiles_per_sample, self.batch
        return pl.BlockSpec((None, None, N_MOD, d),
                            lambda t: (layer, jnp.where(t < lat_tiles, t // tps, batch), 0, 0))

    def rope_spec(self):
        lat_tiles, tps = self.lat_tiles, self.tiles_per_sample
        return pl.BlockSpec((self.tm, LANES),
                            lambda t: (jnp.where(t < lat_tiles, t % tps, tps), 0))

    def rows(self, width, col=0):
        return pl.BlockSpec((self.tm, width), lambda t: (t, col))


def _ffn_chunks(d_ff):
    assert d_ff % MXU_W == 0
    first = (d_ff // MXU_W + 1) // 2 * MXU_W
    return [(0, first), (first, d_ff - first)]


def _ffn(x, mod_ref, g_ref, win_ref, wout_ref, which):
    mod_base, g_base = 6 * which, 4 * which
    d_ff = wout_ref.shape[0]
    shift = mod_ref[mod_base:mod_base + 1, :]
    scale = mod_ref[mod_base + 1:mod_base + 2, :]
    gate = mod_ref[mod_base + 2:mod_base + 3, :]
    h = _modulated(x, g_ref[g_base:g_base + 1, :], shift, scale).astype(BF16)
    chunks = _ffn_chunks(d_ff)
    hidden = [(jnp.dot(h, win_ref[:, lo:lo + n], preferred_element_type=F32),
               jnp.dot(h, win_ref[:, d_ff + lo:d_ff + lo + n], preferred_element_type=F32))
              for lo, n in chunks]
    y = None
    for (lo, n), (gt, up) in zip(chunks, hidden):
        a = (_silu(gt) * up).astype(BF16)
        part = jnp.dot(a, wout_ref[lo:lo + n, :], preferred_element_type=F32)
        y = part if y is None else y + part
    return x + _rms(y, (MACARON_WEIGHT * gate) * g_ref[g_base + 1:g_base + 2, :])


def _ffn_kernel(x_ref, mod_ref, g_ref, win_ref, wout_ref, o_ref):
    o_ref[...] = _ffn(x_ref[...], mod_ref, g_ref, win_ref, wout_ref, 0)


def _ffn_first_kernel(xl_ref, xc_ref, mod_ref, g_ref, win_ref, wout_ref, o_ref, *, lat_tiles):
    x = jnp.where(pl.program_id(0) < lat_tiles, xl_ref[...], xc_ref[...])
    o_ref[...] = _ffn(x, mod_ref, g_ref, win_ref, wout_ref, 0)


def _ffn_sublayer(lay, xs, mod_all, norm_g, w_in, w_out, layer):
    d = xs[0].shape[1]
    if len(xs) == 1:
        kern, x_specs = _ffn_kernel, [lay.rows(d)]
    else:
        lat_tiles, tm = lay.lat_tiles, lay.tm
        kern = functools.partial(_ffn_first_kernel, lat_tiles=lat_tiles)
        x_specs = [pl.BlockSpec((tm, d), lambda t: (jnp.minimum(t, lat_tiles - 1), 0)),
                   pl.BlockSpec((tm, d), lambda t: (jnp.maximum(t - lat_tiles, 0), 0))]
    return pl.pallas_call(
        kern,
        grid=(lay.all_tiles,),
        in_specs=x_specs + [lay.mod_spec(d, layer), _resident(norm_g.shape, (layer,)),
                            _resident(w_in.shape, (layer, 0)), _resident(w_out.shape, (layer, 0))],
        out_specs=lay.rows(d),
        out_shape=jax.ShapeDtypeStruct((lay.n_all, d), F32),
        compiler_params=_params("parallel"),
        name="ffn0",
    )(*xs, mod_all, norm_g, w_in, w_out)


def _mixer_matmul(x_ref, mod_ref, g_ref, w_ref):
    h = _modulated(x_ref[...], g_ref[2:3, :], mod_ref[3:4, :], mod_ref[4:5, :]).astype(BF16)
    return jnp.dot(h, w_ref[...], preferred_element_type=F32)


def _rope(x, cos, sin):
    return x * cos + pltpu.roll(x, LANES // 2, 1) * sin


def _proj_da_kernel(x_ref, mod_ref, g_ref, w_ref, cos_ref, sin_ref, q_ref, k_ref, v_ref, *, q_scale):
    qkv = _mixer_matmul(x_ref, mod_ref, g_ref, w_ref)
    cos, sin = cos_ref[...], sin_ref[...]
    width = q_ref.shape[1]
    cos_q, sin_q = cos * q_scale, sin * q_scale
    for lo in range(0, width, HEAD_W):
        q_ref[:, lo:lo + HEAD_W] = _rope(qkv[:, lo:lo + HEAD_W], cos_q, sin_q).astype(BF16)
        k_ref[:, lo:lo + HEAD_W] = _rope(qkv[:, width + lo:width + lo + HEAD_W], cos, sin).astype(BF16)
    v_ref[...] = qkv[:, 2 * width:].astype(BF16)


def _head_rms(x, g, ones):
    ss = jnp.dot((x * x).astype(BF16), ones, preferred_element_type=F32)
    return x * lax.rsqrt(ss * (1.0 / HEAD_W) + NORM_EPS) * g


def _proj_gqa_kernel(x_ref, mod_ref, g_ref, w_ref, cos_ref, sin_ref, qn_ref, kn_ref,
                     q_ref, k_ref, v_ref, *, q_scale):
    qkv = _mixer_matmul(x_ref, mod_ref, g_ref, w_ref)
    cos, sin = cos_ref[...], sin_ref[...]
    qw, kw = q_ref.shape[1], k_ref.shape[1]
    ones = jnp.ones((HEAD_W, HEAD_W), BF16)
    q_gain = qn_ref[...] * q_scale
    for lo in range(0, qw, HEAD_W):
        xq = _head_rms(qkv[:, lo:lo + HEAD_W], q_gain, ones)
        q_ref[:, lo:lo + HEAD_W] = _rope(xq, cos, sin).astype(BF16)
    for lo in range(0, kw, HEAD_W):
        xk = _head_rms(qkv[:, qw + lo:qw + lo + HEAD_W], kn_ref[...], ones)
        k_ref[:, lo:lo + HEAD_W] = _rope(xk, cos, sin).astype(BF16)
    v_ref[...] = qkv[:, qw + kw:].astype(BF16)


def _proj_hg_kernel(x_ref, mod_ref, g_ref, w_ref, lb_ref, o_ref):
    p = _mixer_matmul(x_ref, mod_ref, g_ref, w_ref)
    d = x_ref.shape[1]
    o_ref[:, 0:d] = _silu(p[:, 0:d])
    for k in range(2):
        lb = lb_ref[k:k + 1, :]
        f = lb + (1.0 - lb) * jax.nn.sigmoid(p[:, (1 + k) * d:(2 + k) * d])
        o_ref[:, (1 + k) * d:(2 + k) * d] = jnp.log2(f)
    o_ref[:, 3 * d:] = p[:, 3 * d:]


def _project(lay, kern, x, mod_all, norm_g, w, layer, j, extra, extra_specs, out_widths, out_dtype,
             name):
    d = x.shape[1]
    outs = pl.pallas_call(
        kern,
        grid=(lay.all_tiles,),
        in_specs=[lay.rows(d), lay.mod_spec(d, layer), _resident(norm_g.shape, (layer,)),
                  _resident(w.shape, (j,))] + extra_specs,
        out_specs=[lay.rows(wd) for wd in out_widths],
        out_shape=[jax.ShapeDtypeStruct((lay.n_all, wd), out_dtype) for wd in out_widths],
        compiler_params=_params("parallel"),
        name=name,
    )(x, mod_all, norm_g, w, *extra)
    return outs


def _nt_dot(a, b):
    return lax.dot_general(a, b, (((1,), (1,)), ((), ())), preferred_element_type=F32)


def _stage_keys(kl_ref, kc_ref, vl_ref, vc_ref, k_s, v_s):
    seq = kl_ref.shape[0]
    for h in range(k_s.shape[0]):
        sl = slice(h * HEAD_W, (h + 1) * HEAD_W)
        k_s[h, 0:seq, :] = kl_ref[:, sl]
        k_s[h, seq:, :] = kc_ref[:, sl]
        v_s[h, 0:seq, 0:HEAD_W] = vl_ref[:, sl]
        v_s[h, seq:, 0:HEAD_W] = vc_ref[:, sl]
        v_s[h, :, HEAD_W:] = jnp.ones((v_s.shape[1], HEAD_W), BF16)


def _key_blocks(start, stop):
    return [(s, min(KEY_BLOCK, stop - s)) for s in range(start, stop, KEY_BLOCK)]


def _softmax_pv(queries, k_s, v_s, blocks, finish):
    steps = [(h, blk) for h in range(len(queries)) for blk in blocks]

    def scores(step):
        h, (start, size) = step
        return _nt_dot(queries[h], k_s[h, start:start + size, :])

    m = acc = None
    ahead = [scores(step) for step in steps[:SCORE_LOOKAHEAD]]
    for n, (h, (start, size)) in enumerate(steps):
        s = ahead.pop(0)
        if n + SCORE_LOOKAHEAD < len(steps):
            ahead.append(scores(steps[n + SCORE_LOOKAHEAD]))
        m_new = jnp.max(s, axis=-1, keepdims=True)
        if m is not None:
            m_new = jnp.maximum(m, m_new)
        e = jnp.exp2(s - m_new).astype(BF16)
        pv = jnp.dot(e, v_s[h, start:start + size, :], preferred_element_type=F32)
        acc = pv if acc is None else acc * jnp.exp2(m - m_new) + pv
        m = m_new
        if (start, size) == blocks[-1]:
            finish(h, acc[:, :HEAD_W] / acc[:, HEAD_W:])
            m = acc = None


def _da_attend(q_ref, k_s, v_s, blocks, lam, gain, o_ref):
    tq = q_ref.shape[0]
    lane = lax.broadcasted_iota(jnp.int32, (1, LANES), 1)
    first_map = (lane % (LANES // 2)) < (LANES // 4)
    queries = []
    for h in range(k_s.shape[0]):
        q = q_ref[:, h * HEAD_W:(h + 1) * HEAD_W]
        zero = jnp.zeros_like(q)
        queries.append(jnp.concatenate([jnp.where(first_map, q, zero),
                                        jnp.where(first_map, zero, q)], axis=0))

    def finish(h, p):
        o_ref[:, h * HEAD_W:(h + 1) * HEAD_W] = _rms(p[:tq] - lam * p[tq:], gain).astype(BF16)

    _softmax_pv(queries, k_s, v_s, blocks, finish)


def _da_attn_kernel(q_ref, kl_ref, kc_ref, vl_ref, vc_ref, lam_ref, g_ref, o_ref, k_s, v_s,
                    *, nq, lam_init):
    lp = lam_ref[...]
    lam = (jnp.exp(jnp.sum(lp[0:1] * lp[1:2], axis=-1, keepdims=True))
           - jnp.exp(jnp.sum(lp[2:3] * lp[3:4], axis=-1, keepdims=True)) + lam_init)
    gain = g_ref[...] * (1.0 - lam_init)
    i = pl.program_id(2)
    seq = kl_ref.shape[0]

    @pl.when(i == 0)
    def _():
        _stage_keys(kl_ref, kc_ref, vl_ref, vc_ref, k_s, v_s)

    total = k_s.shape[1]

    @pl.when(i < nq)
    def _():
        _da_attend(q_ref, k_s, v_s, _key_blocks(0, total), lam, gain, o_ref)

    @pl.when(i >= nq)
    def _():
        _da_attend(q_ref, k_s, v_s, _key_blocks(seq, total), lam, gain, o_ref)


def _gqa_attend(q_ref, k_s, v_s, blocks, o_ref):
    tq = q_ref.shape[0]
    group_w = GQA_GROUP * HEAD_W
    queries = [jnp.concatenate([q_ref[:, h * group_w + g * HEAD_W:h * group_w + (g + 1) * HEAD_W]
                                for g in range(GQA_GROUP)], axis=0) for h in range(k_s.shape[0])]

    def finish(h, o):
        for g in range(GQA_GROUP):
            lo = h * group_w + g * HEAD_W
            o_ref[:, lo:lo + HEAD_W] = o[g * tq:(g + 1) * tq].astype(BF16)

    _softmax_pv(queries, k_s, v_s, blocks, finish)


def _gqa_attn_kernel(q_ref, kl_ref, kc_ref, vl_ref, vc_ref, o_ref, k_s, v_s, *, nq):
    i = pl.program_id(2)
    seq = kl_ref.shape[0]

    @pl.when(i == 0)
    def _():
        _stage_keys(kl_ref, kc_ref, vl_ref, vc_ref, k_s, v_s)

    total = k_s.shape[1]

    @pl.when(i < nq)
    def _():
        _gqa_attend(q_ref, k_s, v_s, _key_blocks(0, total), o_ref)

    @pl.when(i >= nq)
    def _():
        _gqa_attend(q_ref, k_s, v_s, _key_blocks(seq, total), o_ref)


def _attention(lay, kern, q, k, v, extra, extra_specs, *, kv_heads, q_per_kv, tq, need_ctx, name):
    batch, seq, ctx, n_lat = lay.batch, lay.seq, lay.ctx, lay.n_lat
    hp = ATTN_KV_PER_STEP
    assert seq % tq == 0 and ctx % tq == 0 and n_lat % ctx == 0 and kv_heads % hp == 0
    nq, nqc = seq // tq, (ctx // tq if need_ctx else 0)
    lat_rows, ctx_base = n_lat // tq, n_lat // ctx
    q_block_w, kv_block_w = hp * q_per_kv * HEAD_W, hp * HEAD_W

    def q_map(b, h, i):
        return (jnp.where(i < nq, b * nq + i, lat_rows + b * (ctx // tq) + (i - nq)), h)

    kv_lat = pl.BlockSpec((seq, kv_block_w), lambda b, h, i: (b, h))
    kv_ctx = pl.BlockSpec((ctx, kv_block_w), lambda b, h, i: (ctx_base + b, h))
    n_rows = lay.n_all if need_ctx else n_lat
    return pl.pallas_call(
        functools.partial(kern, nq=nq),
        grid=(batch, kv_heads // hp, nq + nqc),
        in_specs=[pl.BlockSpec((tq, q_block_w), q_map), kv_lat, kv_ctx, kv_lat, kv_ctx] + extra_specs,
        out_specs=pl.BlockSpec((tq, q_block_w), q_map),
        out_shape=jax.ShapeDtypeStruct((n_rows, q.shape[1]), BF16),
        scratch_shapes=[pltpu.VMEM((hp, seq + ctx, HEAD_W), BF16),
                        pltpu.VMEM((hp, seq + ctx, 2 * HEAD_W), BF16)],
        compiler_params=_params("parallel", "parallel", "arbitrary"),
        name=name,
    )(q, k, k, v, v, *extra)


def _hg_scan_kernel(q_ref, lf_ref, v_ref, o_ref, st_ref, *, reverse):
    c = HG_CHUNK
    n_sub = q_ref.shape[0] // c
    j = pl.program_id(1)

    @pl.when(j == 0)
    def _():
        st_ref[...] = jnp.zeros_like(st_ref)

    row = lax.broadcasted_iota(jnp.int32, (c, c), 0)
    col = lax.broadcasted_iota(jnp.int32, (c, c), 1)
    later, earlier = (col, row) if reverse else (row, col)
    tri = (earlier <= later).astype(F32)

    levels = []
    half = c // 2
    while half >= 1:
        same = (row // (2 * half)) == (col // (2 * half))
        q_side = (later % (2 * half)) >= half
        k_side = (earlier % (2 * half)) < half
        levels.append((half, same & q_side & k_side))
        half //= 2
    nv = c // SUBLANES
    t_loc = lax.broadcasted_iota(jnp.int32, (nv, SUBLANES, HEAD_W), 1)

    def boundary(cum, half):
        b_row = half if reverse else half - 1
        if 2 * half >= SUBLANES:
            c3 = cum.reshape(c // (2 * half), 2 * half, HEAD_W)
            return jnp.broadcast_to(c3[:, b_row:b_row + 1, :], c3.shape).reshape(c, HEAD_W)
        c3 = cum.reshape(nv, SUBLANES, HEAD_W)
        ref = None
        for blk in range(SUBLANES // (2 * half)):
            r = blk * 2 * half + b_row
            cand = jnp.broadcast_to(c3[:, r:r + 1, :], c3.shape)
            ref = cand if ref is None else jnp.where(t_loc >= blk * 2 * half, cand, ref)
        return ref.reshape(c, HEAD_W)

    heads = range(q_ref.shape[1] // HEAD_W)
    end = 0 if reverse else c - 1
    states = [st_ref[h] for h in heads]

    for sub in (reversed(range(n_sub)) if reverse else range(n_sub)):
        rows = slice(sub * c, (sub + 1) * c)
        _hg_chunk(q_ref, lf_ref, v_ref, o_ref, rows, states, tri, row == col, levels, boundary, end)

    for h in heads:
        st_ref[h] = states[h]


def _hg_chunk(q_ref, lf_ref, v_ref, o_ref, rows, states, tri, diagonal, levels, boundary, end):
    cum_all = jnp.dot(tri, lf_ref[rows, :], preferred_element_type=F32,
                      precision=lax.Precision.HIGHEST)
    staged = []
    for h in range(len(states)):
        sl = slice(h * HEAD_W, (h + 1) * HEAD_W)
        q, lf, v = q_ref[rows, sl], lf_ref[rows, sl], v_ref[rows, sl]
        cum = cum_all[:, sl]
        f = jnp.exp2(lf)
        k = 1.0 - f
        v_b, k_b = v.astype(BF16), k.astype(BF16)
        att = jnp.where(diagonal, _nt_dot(q.astype(BF16), k_b), 0.0)
        for half, mask in levels:
            if half == 1:
                qt, kt = (q * f).astype(BF16), k_b
            else:
                decay = jnp.exp2(-jnp.abs(cum - boundary(cum, half)))
                qt, kt = (q * decay).astype(BF16), (k * decay).astype(BF16)
            att = jnp.where(mask, _nt_dot(qt, kt), att)
        total = cum[end:end + 1, :]
        st = states[h]
        inter = _nt_dot((q * jnp.exp2(cum)).astype(BF16), st.astype(BF16))
        kd = (k * jnp.exp2(total - cum)).astype(BF16)
        upd = lax.dot_general(v_b, kd, (((0,), (0,)), ((), ())), preferred_element_type=F32)
        states[h] = st * jnp.exp2(total) + upd
        staged.append((sl, att.astype(BF16), v_b, inter))

    for sl, att_b, v_b, inter in staged:
        o_ref[rows, sl] = jnp.dot(att_b, v_b, preferred_element_type=F32) + inter


def _hg_scan(lay, p, d, reverse):
    batch, seq, ctx, n_lat = lay.batch, lay.seq, lay.ctx, lay.n_lat
    c = HG_CHUNK * HG_CHUNKS_PER_STEP
    assert seq % c == 0 and ctx % c == 0
    nlc, ncc = seq // c, ctx // c
    ctx_base = n_lat // c

    def blk(b, j):
        c_ctx = (ncc - 1 - j) if reverse else j
        c_lat = (nlc - 1 - (j - ncc)) if reverse else (j - ncc)
        return jnp.where(j < ncc, ctx_base + b * ncc + c_ctx, b * nlc + c_lat)

    def col_spec(colblk):
        return pl.BlockSpec((c, d), lambda b, j: (blk(b, j), colblk))

    return pl.pallas_call(
        functools.partial(_hg_scan_kernel, reverse=reverse),
        grid=(batch, ncc + nlc),
        in_specs=[col_spec(0), col_spec(2 if reverse else 1), col_spec(3)],
        out_specs=col_spec(0),
        out_shape=jax.ShapeDtypeStruct((lay.n_all, d), F32),
        scratch_shapes=[pltpu.VMEM((d // HEAD_W, HEAD_W, HEAD_W), F32)],
        compiler_params=_params("arbitrary", "arbitrary"),
        name="hg_scan_bwd" if reverse else "hg_scan_fwd",
    )(p, p, p)


def _post_mixer(x, y_b, mod_ref, g_ref, wo_ref, win_ref, wout_ref):
    y = jnp.dot(y_b, wo_ref[...], preferred_element_type=F32)
    x = x + _rms(y, mod_ref[5:6, :] * g_ref[3:4, :])
    return _ffn(x, mod_ref, g_ref, win_ref, wout_ref, 1)


def _post_kernel(x_ref, y_ref, mod_ref, g_ref, wo_ref, win_ref, wout_ref, o_ref):
    o_ref[...] = _post_mixer(x_ref[...], y_ref[...], mod_ref, g_ref, wo_ref, win_ref, wout_ref)


def _post_hg_kernel(x_ref, of_ref, ob_ref, gate_ref, hn_ref, mod_ref, g_ref, wo_ref, win_ref,
                    wout_ref, o_ref):
    o = of_ref[...] + ob_ref[...]
    gate = gate_ref[...]
    parts = []
    for h in range(o.shape[1] // HEAD_W):
        sl = slice(h * HEAD_W, (h + 1) * HEAD_W)
        parts.append((_rms(o[:, sl], hn_ref[...]) * _silu(gate[:, sl])).astype(BF16))
    o_ref[...] = _post_mixer(x_ref[...], jnp.concatenate(parts, axis=1), mod_ref, g_ref, wo_ref,
                             win_ref, wout_ref)


def _post_sublayer(lay, kern, x, ys, y_specs, extra, extra_specs, mod_all, norm_g, w_o, w_in, w_out,
                   layer, j, n_tiles, name):
    d = x.shape[1]
    return pl.pallas_call(
        kern,
        grid=(n_tiles,),
        in_specs=[lay.rows(d)] + y_specs + extra_specs
                 + [lay.mod_spec(d, layer), _resident(norm_g.shape, (layer,)),
                    _resident(w_o.shape, (j,)), _resident(w_in.shape, (layer, 1)),
                    _resident(w_out.shape, (layer, 1))],
        out_specs=lay.rows(d),
        out_shape=jax.ShapeDtypeStruct((n_tiles * lay.tm, d), F32),
        compiler_params=_params("parallel"),
        name=name,
    )(x, *ys, *extra, mod_all, norm_g, w_o, w_in, w_out)


def _rope_tables(rows, head_dim, n_id_rows, maps):
    pairs = head_dim // 4
    inv_freq = jnp.power(ROPE_THETA, -jnp.arange(pairs, dtype=F32) / pairs)
    r = jnp.repeat(jnp.arange(rows, dtype=F32), GRID_W)
    col = jnp.tile(jnp.arange(GRID_W, dtype=F32), rows)
    ang = jnp.concatenate([r[:, None] * inv_freq, col[:, None] * inv_freq], axis=-1)
    cos, sin = jnp.cos(ang), jnp.sin(ang)
    cos = jnp.tile(cos, (1, 2 * maps))
    sin = jnp.concatenate([-jnp.tile(sin, (1, maps)), jnp.tile(sin, (1, maps))], axis=-1)
    cos = jnp.concatenate([cos, jnp.ones((n_id_rows, LANES), F32)], axis=0)
    sin = jnp.concatenate([sin, jnp.zeros((n_id_rows, LANES), F32)], axis=0)
    return cos, sin


def _da_head_perm():
    p = np.arange(2)[:, None, None]
    m = np.arange(2)[None, :, None]
    j = np.arange(HEAD_W // 4)[None, None, :]
    return (m * (HEAD_W // 2) + 2 * j + p).reshape(-1)


def _gqa_head_perm():
    p = np.arange(2)[:, None]
    j = np.arange(HEAD_W // 2)[None, :]
    return (2 * j + p).reshape(-1)


def _permute_heads(n_blocks, perm, n_tail):
    idx = (np.arange(n_blocks)[:, None] * HEAD_W + perm[None, :]).reshape(-1)
    return np.concatenate([idx, n_blocks * HEAD_W + np.arange(n_tail)])


def kernel(x, c, ctx, c_ctx, w_mod, b_mod, norm_g, ffn_w_in, ffn_w_out, da_w_qkv, da_lambda, da_subln, da_w_o, hg_w_in, hg_lower_bound, hg_norm, hg_w_o, gqa_w_qkv, gqa_q_norm, gqa_k_norm, gqa_w_o):
    batch, seq, d = x.shape
    n_ctx = ctx.shape[1]
    depth = w_mod.shape[0]
    assert d == N_HEADS * HEAD_W
    lay = _Layout(batch, seq, n_ctx, TOKEN_TILE)
    lay_p = _Layout(batch, seq, n_ctx, PROJ_TILE)
    rows = seq // GRID_W

    pad = (-(batch + 1)) % SUBLANES
    c_rows = jnp.concatenate([c, c_ctx[None, :], jnp.zeros((pad, d), F32)], axis=0)
    mod_all = _modulation(c_rows, w_mod, b_mod)[:, :batch + 1].reshape(depth, batch + 1, N_MOD, d)

    w_in, w_out = ffn_w_in.astype(BF16), ffn_w_out.astype(BF16)
    da_cos, da_sin = _rope_tables(rows, HEAD_W // 2, lay_p.tm, 2)
    gqa_cos, gqa_sin = _rope_tables(rows, HEAD_W, lay_p.tm, 1)
    da_w = da_w_qkv.astype(BF16)[:, :, _permute_heads(2 * N_HEADS, _da_head_perm(), d)]
    gqa_perm = _gqa_head_perm()
    kvw = d // GQA_GROUP
    gqa_w = gqa_w_qkv.astype(BF16)[:, :, _permute_heads(N_HEADS + N_HEADS // GQA_GROUP, gqa_perm, kvw)]
    gqa_qn, gqa_kn = gqa_q_norm[:, None, gqa_perm], gqa_k_norm[:, None, gqa_perm]
    da_wo, hg_wo, gqa_wo = da_w_o.astype(BF16), hg_w_o.astype(BF16), gqa_w_o.astype(BF16)
    hg_w = hg_w_in.astype(BF16)
    lb_table = jnp.cumsum(jax.nn.softmax(hg_lower_bound.astype(F32), axis=1), axis=1)
    lb_table = jnp.swapaxes(lb_table - lb_table[:, :1], 0, 1)
    da_gain, hg_gain = da_subln[:, None, :], hg_norm[:, None, :]

    xs = [x.reshape(batch * seq, d), ctx.reshape(batch * n_ctx, d)]
    for i in range(depth):
        kind, j = i % N_MIXERS, i // N_MIXERS
        need_ctx = i < depth - 1
        xa = _ffn_sublayer(lay, xs, mod_all, norm_g, w_in, w_out, i)
        post = functools.partial(_post_sublayer, mod_all=mod_all, norm_g=norm_g, w_in=w_in,
                                 w_out=w_out, layer=i, j=j)

        if kind == 0:
            lam_init = 0.8 - 0.6 * math.exp(-0.3 * i)
            q, k, v = _project(
                lay_p, functools.partial(_proj_da_kernel, q_scale=LOG2_E * (HEAD_W // 2) ** -0.5),
                xa, mod_all, norm_g, da_w, i, j, [da_cos, da_sin],
                [lay_p.rope_spec(), lay_p.rope_spec()],
                [d, d, d], BF16, "proj_da")
            y = _attention(
                lay, functools.partial(_da_attn_kernel, lam_init=lam_init), q, k, v,
                [da_lambda, da_gain],
                [_resident(da_lambda.shape, (j,)), _resident(da_gain.shape, (j,))],
                kv_heads=N_HEADS, q_per_kv=1, tq=DA_Q_TILE, need_ctx=need_ctx, name="attn_da")
            xa = post(lay, _post_kernel, xa, [y], [lay.rows(d)], [], [], w_o=da_wo,
                      n_tiles=lay.all_tiles if need_ctx else lay.lat_tiles, name="post_da")
        elif kind == 1:
            p, = _project(lay_p, _proj_hg_kernel, xa, mod_all, norm_g, hg_w, i, j,
                          [lb_table], [_resident(lb_table.shape, (i,))], [5 * d], F32, "proj_hg")
            o_f = _hg_scan(lay, p, d, reverse=False)
            o_b = _hg_scan(lay, p, d, reverse=True)
            xa = post(lay_p, _post_hg_kernel, xa, [o_f, o_b, p],
                      [lay_p.rows(d), lay_p.rows(d), lay_p.rows(d, 4)],
                      [hg_gain], [_resident(hg_gain.shape, (j,))], w_o=hg_wo,
                      n_tiles=lay_p.all_tiles if need_ctx else lay_p.lat_tiles, name="post_hg")
        else:
            q, k, v = _project(
                lay_p, functools.partial(_proj_gqa_kernel, q_scale=LOG2_E * HEAD_W ** -0.5),
                xa, mod_all, norm_g, gqa_w, i, j,
                [gqa_cos, gqa_sin, gqa_qn, gqa_kn],
                [lay_p.rope_spec(), lay_p.rope_spec(), _resident(gqa_qn.shape, (j,)),
                 _resident(gqa_kn.shape, (j,))],
                [d, kvw, kvw], BF16, "proj_gqa")
            y = _attention(lay, _gqa_attn_kernel, q, k, v, [], [],
                           kv_heads=N_HEADS // GQA_GROUP, q_per_kv=GQA_GROUP,
                           tq=GQA_Q_TILE, need_ctx=need_ctx, name="attn_gqa")
            xa = post(lay, _post_kernel, xa, [y], [lay.rows(d)], [], [], w_o=gqa_wo,
                      n_tiles=lay.all_tiles if need_ctx else lay.lat_tiles, name="post_gqa")
        xs = [xa]
    return xa.reshape(batch, seq, d)
```

```python
import functools
import math

import numpy as np
import jax
import jax.numpy as jnp
from jax import lax
from jax.experimental import pallas as pl
from jax.experimental.pallas import tpu as pltpu

F32 = jnp.float32
BF16 = jnp.bfloat16

NORM_EPS = 1e-6
ROPE_THETA = 10000.0
GRID_W = 64
N_MOD = 9
N_MIXERS = 3
MACARON_WEIGHT = 0.5
LOG2_E = math.log2(math.e)
LANES = 128
SUBLANES = 8
MXU_W = 256
HEAD_W = 128
N_HEADS = 8
GQA_GROUP = 4
VMEM_LIMIT = 56 * 1024 * 1024

TOKEN_TILE = 1024
PROJ_TILE = 512
DA_Q_TILE = 512
GQA_Q_TILE = 256
KEY_BLOCK = 512
ATTN_KV_PER_STEP = 2
SCORE_LOOKAHEAD = 1
HG_CHUNK = 64
HG_CHUNKS_PER_STEP = 4


def _params(*sem):
    return pltpu.CompilerParams(dimension_semantics=sem, vmem_limit_bytes=VMEM_LIMIT)


def _resident(shape, lead=()):
    block = (None,) * len(lead) + tuple(shape[len(lead):])
    index = tuple(lead) + (0,) * (len(shape) - len(lead))
    return pl.BlockSpec(block, lambda *_: index, pipeline_mode=pl.Buffered(1))


def _rms(x, g):
    return x * lax.rsqrt(jnp.mean(x * x, axis=-1, keepdims=True) + NORM_EPS) * g


def _modulated(x, g, shift, scale):
    return _rms(x, g * (1.0 + scale)) + shift


def _silu(x):
    return x * jax.nn.sigmoid(x)


def _mod_kernel(c_ref, w_ref, b_ref, o_ref):
    s = _silu(c_ref[...])
    o_ref[0] = jnp.dot(s, w_ref[0], preferred_element_type=F32,
                       precision=lax.Precision.HIGHEST) + b_ref[0]


def _modulation(c_rows, w_mod, b_mod):
    depth, d, n = w_mod.shape
    rows = c_rows.shape[0]
    tn = n // 8
    return pl.pallas_call(
        _mod_kernel,
        grid=(depth, n // tn),
        in_specs=[pl.BlockSpec((rows, d), lambda i, j: (0, 0)),
                  pl.BlockSpec((1, d, tn), lambda i, j: (i, 0, j)),
                  pl.BlockSpec((1, 1, tn), lambda i, j: (i, 0, j))],
        out_specs=pl.BlockSpec((1, rows, tn), lambda i, j: (i, 0, j)),
        out_shape=jax.ShapeDtypeStruct((depth, rows, n), F32),
        compiler_params=_params("arbitrary", "arbitrary"),
        name="modulation",
    )(c_rows, w_mod, b_mod.reshape(depth, 1, n))


class _Layout:
    def __init__(self, batch, seq, ctx, tm=TOKEN_TILE):
        self.batch, self.seq, self.ctx = batch, seq, ctx
        self.n_lat = batch * seq
        self.n_all = self.n_lat + batch * ctx
        assert seq % tm == 0 and (batch * ctx) % tm == 0
        self.tm = tm
        self.lat_tiles = self.n_lat // tm
        self.all_tiles = self.n_all // tm
        self.tiles_per_sample = seq // tm

    def mod_spec(self, d, layer):
        lat_tiles, tps, batch = self.lat_tiles, self.tiles_per_sample, self.batch
        return pl.BlockSpec((None, None, N_MOD, d),
                            lambda t: (layer, jnp.where(t < lat_tiles, t // tps, batch), 0, 0))

    def rope_spec(self):
        lat_tiles, tps = self.lat_tiles, self.tiles_per_sample
        return pl.BlockSpec((self.tm, LANES),
                            lambda t: (jnp.where(t < lat_tiles, t % tps, tps), 0))

    def rows(self, width, col=0):
        return pl.BlockSpec((self.tm, width), lambda t: (t, col))


def _ffn_chunks(d_ff):
    assert d_ff % MXU_W == 0
    first = (d_ff // MXU_W + 1) // 2 * MXU_W
    return [(0, first), (first, d_ff - first)]


def _ffn(x, mod_ref, g_ref, win_ref, wout_ref, which):
    mod_base, g_base = 6 * which, 4 * which
    d_ff = wout_ref.shape[0]
    shift = mod_ref[mod_base:mod_base + 1, :]
    scale = mod_ref[mod_base + 1:mod_base + 2, :]
    gate = mod_ref[mod_base + 2:mod_base + 3, :]
    h = _modulated(x, g_ref[g_base:g_base + 1, :], shift, scale).astype(BF16)
    chunks = _ffn_chunks(d_ff)
    hidden = [(jnp.dot(h, win_ref[:, lo:lo + n], preferred_element_type=F32),
               jnp.dot(h, win_ref[:, d_ff + lo:d_ff + lo + n], preferred_element_type=F32))
              for lo, n in chunks]
    y = None
    for (lo, n), (gt, up) in zip(chunks, hidden):
        a = (_silu(gt) * up).astype(BF16)
        part = jnp.dot(a, wout_ref[lo:lo + n, :], preferred_element_type=F32)
        y = part if y is None else y + part
    return x + _rms(y, (MACARON_WEIGHT * gate) * g_ref[g_base + 1:g_base + 2, :])


def _ffn_kernel(x_ref, mod_ref, g_ref, win_ref, wout_ref, o_ref):
    o_ref[...] = _ffn(x_ref[...], mod_ref, g_ref, win_ref, wout_ref, 0)


def _ffn_first_kernel(xl_ref, xc_ref, mod_ref, g_ref, win_ref, wout_ref, o_ref, *, lat_tiles):
    x = jnp.where(pl.program_id(0) < lat_tiles, xl_ref[...], xc_ref[...])
    o_ref[...] = _ffn(x, mod_ref, g_ref, win_ref, wout_ref, 0)


def _ffn_sublayer(lay, xs, mod_all, norm_g, w_in, w_out, layer):
    d = xs[0].shape[1]
    if len(xs) == 1:
        kern, x_specs = _ffn_kernel, [lay.rows(d)]
    else:
        lat_tiles, tm = lay.lat_tiles, lay.tm
        kern = functools.partial(_ffn_first_kernel, lat_tiles=lat_tiles)
        x_specs = [pl.BlockSpec((tm, d), lambda t: (jnp.minimum(t, lat_tiles - 1), 0)),
                   pl.BlockSpec((tm, d), lambda t: (jnp.maximum(t - lat_tiles, 0), 0))]
    return pl.pallas_call(
        kern,
        grid=(lay.all_tiles,),
        in_specs=x_specs + [lay.mod_spec(d, layer), _resident(norm_g.shape, (layer,)),
                            _resident(w_in.shape, (layer, 0)), _resident(w_out.shape, (layer, 0))],
        out_specs=lay.rows(d),
        out_shape=jax.ShapeDtypeStruct((lay.n_all, d), F32),
        compiler_params=_params("parallel"),
        name="ffn0",
    )(*xs, mod_all, norm_g, w_in, w_out)


def _mixer_matmul(x_ref, mod_ref, g_ref, w_ref):
    h = _modulated(x_ref[...], g_ref[2:3, :], mod_ref[3:4, :], mod_ref[4:5, :]).astype(BF16)
    return jnp.dot(h, w_ref[...], preferred_element_type=F32)


def _rope(x, cos, sin):
    return x * cos + pltpu.roll(x, LANES // 2, 1) * sin


def _proj_da_kernel(x_ref, mod_ref, g_ref, w_ref, cos_ref, sin_ref, q_ref, k_ref, v_ref, *, q_scale):
    qkv = _mixer_matmul(x_ref, mod_ref, g_ref, w_ref)
    cos, sin = cos_ref[...], sin_ref[...]
    width = q_ref.shape[1]
    cos_q, sin_q = cos * q_scale, sin * q_scale
    for lo in range(0, width, HEAD_W):
        q_ref[:, lo:lo + HEAD_W] = _rope(qkv[:, lo:lo + HEAD_W], cos_q, sin_q).astype(BF16)
        k_ref[:, lo:lo + HEAD_W] = _rope(qkv[:, width + lo:width + lo + HEAD_W], cos, sin).astype(BF16)
    v_ref[...] = qkv[:, 2 * width:].astype(BF16)


def _head_rms(x, g, ones):
    ss = jnp.dot((x * x).astype(BF16), ones, preferred_element_type=F32)
    return x * lax.rsqrt(ss * (1.0 / HEAD_W) + NORM_EPS) * g


def _proj_gqa_kernel(x_ref, mod_ref, g_ref, w_ref, cos_ref, sin_ref, qn_ref, kn_ref,
                     q_ref, k_ref, v_ref, *, q_scale):
    qkv = _mixer_matmul(x_ref, mod_ref, g_ref, w_ref)
    cos, sin = cos_ref[...], sin_ref[...]
    qw, kw = q_ref.shape[1], k_ref.shape[1]
    ones = jnp.ones((HEAD_W, HEAD_W), BF16)
    q_gain = qn_ref[...] * q_scale
    for lo in range(0, qw, HEAD_W):
        xq = _head_rms(qkv[:, lo:lo + HEAD_W], q_gain, ones)
        q_ref[:, lo:lo + HEAD_W] = _rope(xq, cos, sin).astype(BF16)
    for lo in range(0, kw, HEAD_W):
        xk = _head_rms(qkv[:, qw + lo:qw + lo + HEAD_W], kn_ref[...], ones)
        k_ref[:, lo:lo + HEAD_W] = _rope(xk, cos, sin).astype(BF16)
    v_ref[...] = qkv[:, qw + kw:].astype(BF16)


def _proj_hg_kernel(x_ref, mod_ref, g_ref, w_ref, lb_ref, o_ref):
    p = _mixer_matmul(x_ref, mod_ref, g_ref, w_ref)
    d = x_ref.shape[1]
    o_ref[:, 0:d] = _silu(p[:, 0:d])
    for k in range(2):
        lb = lb_ref[k:k + 1, :]
        f = lb + (1.0 - lb) * jax.nn.sigmoid(p[:, (1 + k) * d:(2 + k) * d])
        o_ref[:, (1 + k) * d:(2 + k) * d] = jnp.log2(f)
    o_ref[:, 3 * d:] = p[:, 3 * d:]


def _project(lay, kern, x, mod_all, norm_g, w, layer, j, extra, extra_specs, out_widths, out_dtype,
             name):
    d = x.shape[1]
    outs = pl.pallas_call(
        kern,
        grid=(lay.all_tiles,),
        in_specs=[lay.rows(d), lay.mod_spec(d, layer), _resident(norm_g.shape, (layer,)),
                  _resident(w.shape, (j,))] + extra_specs,
        out_specs=[lay.rows(wd) for wd in out_widths],
        out_shape=[jax.ShapeDtypeStruct((lay.n_all, wd), out_dtype) for wd in out_widths],
        compiler_params=_params("parallel"),
        name=name,
    )(x, mod_all, norm_g, w, *extra)
    return outs


def _nt_dot(a, b):
    return lax.dot_general(a, b, (((1,), (1,)), ((), ())), preferred_element_type=F32)


def _stage_keys(k_refs, v_refs, k_s, v_s):
    for h in range(k_s.shape[0]):
        sl = slice(h * HEAD_W, (h + 1) * HEAD_W)
        row = 0
        for k_ref, v_ref in zip(k_refs, v_refs):
            n = k_ref.shape[0]
            k_s[h, row:row + n, :] = k_ref[:, sl]
            v_s[h, row:row + n, 0:HEAD_W] = v_ref[:, sl]
            row += n
        v_s[h, :, HEAD_W:] = jnp.ones((v_s.shape[1], HEAD_W), BF16)


def _key_blocks(start, stop):
    return [(s, min(KEY_BLOCK, stop - s)) for s in range(start, stop, KEY_BLOCK)]


def _softmax_pv(queries, k_s, v_s, blocks, finish):
    steps = [(h, blk) for h in range(len(queries)) for blk in blocks]

    def scores(step):
        h, (start, size) = step
        return _nt_dot(queries[h], k_s[h, start:start + size, :])

    m = acc = None
    ahead = [scores(step) for step in steps[:SCORE_LOOKAHEAD]]
    for n, (h, (start, size)) in enumerate(steps):
        s = ahead.pop(0)
        if n + SCORE_LOOKAHEAD < len(steps):
            ahead.append(scores(steps[n + SCORE_LOOKAHEAD]))
        m_new = jnp.max(s, axis=-1, keepdims=True)
        if m is not None:
            m_new = jnp.maximum(m, m_new)
        e = jnp.exp2(s - m_new).astype(BF16)
        pv = jnp.dot(e, v_s[h, start:start + size, :], preferred_element_type=F32)
        acc = pv if acc is None else acc * jnp.exp2(m - m_new) + pv
        m = m_new
        if (start, size) == blocks[-1]:
            finish(h, acc[:, :HEAD_W] / acc[:, HEAD_W:])
            m = acc = None


def _da_attend(q_ref, k_s, v_s, blocks, lam, gain, o_ref):
    tq = q_ref.shape[0]
    lane = lax.broadcasted_iota(jnp.int32, (1, LANES), 1)
    first_map = (lane % (LANES // 2)) < (LANES // 4)
    queries = []
    for h in range(k_s.shape[0]):
        q = q_ref[:, h * HEAD_W:(h + 1) * HEAD_W]
        zero = jnp.zeros_like(q)
        queries.append(jnp.concatenate([jnp.where(first_map, q, zero),
                                        jnp.where(first_map, zero, q)], axis=0))

    def finish(h, p):
        o_ref[:, h * HEAD_W:(h + 1) * HEAD_W] = _rms(p[:tq] - lam * p[tq:], gain).astype(BF16)

    _softmax_pv(queries, k_s, v_s, blocks, finish)


def _split_refs(refs, n_seg, n_extra):
    keys, values = refs[1:1 + n_seg], refs[1 + n_seg:1 + 2 * n_seg]
    extra = refs[1 + 2 * n_seg:1 + 2 * n_seg + n_extra]
    return (refs[0], keys, values, extra) + tuple(refs[-3:])


def _da_attn_kernel(*refs, n_seg, lam_init):
    q_ref, k_refs, v_refs, (lam_ref, g_ref), o_ref, k_s, v_s = _split_refs(refs, n_seg, 2)
    lp = lam_ref[...]
    lam = (jnp.exp(jnp.sum(lp[0:1] * lp[1:2], axis=-1, keepdims=True))
           - jnp.exp(jnp.sum(lp[2:3] * lp[3:4], axis=-1, keepdims=True)) + lam_init)
    gain = g_ref[...] * (1.0 - lam_init)

    @pl.when(pl.program_id(2) == 0)
    def _():
        _stage_keys(k_refs, v_refs, k_s, v_s)

    _da_attend(q_ref, k_s, v_s, _key_blocks(0, k_s.shape[1]), lam, gain, o_ref)


def _gqa_attend(q_ref, k_s, v_s, blocks, o_ref):
    tq = q_ref.shape[0]
    group_w = GQA_GROUP * HEAD_W
    queries = [jnp.concatenate([q_ref[:, h * group_w + g * HEAD_W:h * group_w + (g + 1) * HEAD_W]
                                for g in range(GQA_GROUP)], axis=0) for h in range(k_s.shape[0])]

    def finish(h, o):
        for g in range(GQA_GROUP):
            lo = h * group_w + g * HEAD_W
            o_ref[:, lo:lo + HEAD_W] = o[g * tq:(g + 1) * tq].astype(BF16)

    _softmax_pv(queries, k_s, v_s, blocks, finish)


def _gqa_attn_kernel(*refs, n_seg):
    q_ref, k_refs, v_refs, _, o_ref, k_s, v_s = _split_refs(refs, n_seg, 0)

    @pl.when(pl.program_id(2) == 0)
    def _():
        _stage_keys(k_refs, v_refs, k_s, v_s)

    _gqa_attend(q_ref, k_s, v_s, _key_blocks(0, k_s.shape[1]), o_ref)


def _attention(lay, kern, q, k, v, extra, extra_specs, *, kv_heads, q_per_kv, tq, need_ctx, name):
    batch, seq, ctx, n_lat = lay.batch, lay.seq, lay.ctx, lay.n_lat
    hp = ATTN_KV_PER_STEP
    tc = min(tq, ctx)
    assert seq % tq == 0 and ctx % tc == 0 and n_lat % ctx == 0 and kv_heads % hp == 0
    q_block_w, kv_block_w = hp * q_per_kv * HEAD_W, hp * HEAD_W
    kv_lat = pl.BlockSpec((seq, kv_block_w), lambda b, h, i: (b, h))
    kv_ctx = pl.BlockSpec((ctx, kv_block_w), lambda b, h, i: (n_lat // ctx + b, h))
    n_rows = lay.n_all if need_ctx else n_lat

    def call(tile, tiles, first_tile, kv_specs, n_keys, earlier, call_name):
        q_spec = pl.BlockSpec((tile, q_block_w), lambda b, h, i: (first_tile(b) + i, h))
        n_seg = len(kv_specs)
        inputs = [q] + [k] * n_seg + [v] * n_seg + list(extra)
        in_specs = [q_spec] + kv_specs + kv_specs + extra_specs
        aliases = {}
        if earlier is not None:
            aliases = {len(inputs): 0}
            inputs.append(earlier)
            in_specs.append(pl.BlockSpec(memory_space=pl.ANY))
        return pl.pallas_call(
            functools.partial(kern, n_seg=n_seg),
            grid=(batch, kv_heads // hp, tiles),
            in_specs=in_specs,
            out_specs=q_spec,
            out_shape=jax.ShapeDtypeStruct((n_rows, q.shape[1]), BF16),
            scratch_shapes=[pltpu.VMEM((hp, n_keys, HEAD_W), BF16),
                            pltpu.VMEM((hp, n_keys, 2 * HEAD_W), BF16)],
            input_output_aliases=aliases,
            compiler_params=_params("parallel", "parallel", "arbitrary"),
            name=call_name,
        )(*inputs)

    y = call(tq, seq // tq, lambda b: b * (seq // tq), [kv_lat, kv_ctx], seq + ctx, None, name)
    if need_ctx:
        y = call(tc, ctx // tc, lambda b: n_lat // tc + b * (ctx // tc), [kv_ctx], ctx, y,
                 name + "_ctx")
    return y


def _hg_direction(c, reverse):
    row = lax.broadcasted_iota(jnp.int32, (c, c), 0)
    col = lax.broadcasted_iota(jnp.int32, (c, c), 1)
    later, earlier = (col, row) if reverse else (row, col)
    tri = (earlier <= later).astype(F32)

    levels = []
    half = c // 2
    while half >= 1:
        same = (row // (2 * half)) == (col // (2 * half))
        q_side = (later % (2 * half)) >= half
        k_side = (earlier % (2 * half)) < half
        levels.append((half, same & q_side & k_side))
        half //= 2
    nv = c // SUBLANES
    t_loc = lax.broadcasted_iota(jnp.int32, (nv, SUBLANES, HEAD_W), 1)

    def boundary(cum, half):
        b_row = half if reverse else half - 1
        if 2 * half >= SUBLANES:
            c3 = cum.reshape(c // (2 * half), 2 * half, HEAD_W)
            return jnp.broadcast_to(c3[:, b_row:b_row + 1, :], c3.shape).reshape(c, HEAD_W)
        c3 = cum.reshape(nv, SUBLANES, HEAD_W)
        ref = None
        for blk in range(SUBLANES // (2 * half)):
            r = blk * 2 * half + b_row
            cand = jnp.broadcast_to(c3[:, r:r + 1, :], c3.shape)
            ref = cand if ref is None else jnp.where(t_loc >= blk * 2 * half, cand, ref)
        return ref.reshape(c, HEAD_W)

    return tri, row == col, levels, boundary, (0 if reverse else c - 1)


def _hg_scan_kernel(qf_ref, lff_ref, vf_ref, qb_ref, lfb_ref, vb_ref, of_ref, ob_ref, stf_ref, stb_ref):
    c = HG_CHUNK
    n_sub = qf_ref.shape[0] // c

    @pl.when(pl.program_id(1) == 0)
    def _():
        stf_ref[...] = jnp.zeros_like(stf_ref)
        stb_ref[...] = jnp.zeros_like(stb_ref)

    fwd, bwd = _hg_direction(c, False), _hg_direction(c, True)
    heads = range(qf_ref.shape[1] // HEAD_W)
    states_f = [stf_ref[h] for h in heads]
    states_b = [stb_ref[h] for h in heads]

    for sub in range(n_sub):
        rows = slice(sub * c, (sub + 1) * c)
        _hg_chunk(qf_ref, lff_ref, vf_ref, of_ref, rows, states_f, *fwd)
        rows = slice((n_sub - 1 - sub) * c, (n_sub - sub) * c)
        _hg_chunk(qb_ref, lfb_ref, vb_ref, ob_ref, rows, states_b, *bwd)

    for h in heads:
        stf_ref[h] = states_f[h]
        stb_ref[h] = states_b[h]


def _hg_chunk(q_ref, lf_ref, v_ref, o_ref, rows, states, tri, diagonal, levels, boundary, end):
    cum_all = jnp.dot(tri, lf_ref[rows, :], preferred_element_type=F32,
                      precision=lax.Precision.HIGHEST)
    staged = []
    for h in range(len(states)):
        sl = slice(h * HEAD_W, (h + 1) * HEAD_W)
        q, lf, v = q_ref[rows, sl], lf_ref[rows, sl], v_ref[rows, sl]
        cum = cum_all[:, sl]
        f = jnp.exp2(lf)
        k = 1.0 - f
        v_b, k_b = v.astype(BF16), k.astype(BF16)
        att = jnp.where(diagonal, _nt_dot(q.astype(BF16), k_b), 0.0)
        for half, mask in levels:
            if half == 1:
                qt, kt = (q * f).astype(BF16), k_b
            else:
                decay = jnp.exp2(-jnp.abs(cum - boundary(cum, half)))
                qt, kt = (q * decay).astype(BF16), (k * decay).astype(BF16)
            att = jnp.where(mask, _nt_dot(qt, kt), att)
        total = cum[end:end + 1, :]
        st = states[h]
        inter = _nt_dot((q * jnp.exp2(cum)).astype(BF16), st.astype(BF16))
        kd = (k * jnp.exp2(total - cum)).astype(BF16)
        upd = lax.dot_general(v_b, kd, (((0,), (0,)), ((), ())), preferred_element_type=F32)
        states[h] = st * jnp.exp2(total) + upd
        staged.append((sl, att.astype(BF16), v_b, inter))

    for sl, att_b, v_b, inter in staged:
        o_ref[rows, sl] = jnp.dot(att_b, v_b, preferred_element_type=F32) + inter


def _hg_scan(lay, p, d):
    batch, seq, ctx, n_lat = lay.batch, lay.seq, lay.ctx, lay.n_lat
    c = HG_CHUNK * HG_CHUNKS_PER_STEP
    assert seq % c == 0 and ctx % c == 0
    nlc, ncc = seq // c, ctx // c
    ctx_base = n_lat // c

    def blk(b, j, reverse):
        c_ctx = (ncc - 1 - j) if reverse else j
        c_lat = (nlc - 1 - (j - ncc)) if reverse else (j - ncc)
        return jnp.where(j < ncc, ctx_base + b * ncc + c_ctx, b * nlc + c_lat)

    def col_spec(colblk, reverse):
        return pl.BlockSpec((c, d), lambda b, j: (blk(b, j, reverse), colblk))

    state = pltpu.VMEM((d // HEAD_W, HEAD_W, HEAD_W), F32)
    return pl.pallas_call(
        _hg_scan_kernel,
        grid=(batch, ncc + nlc),
        in_specs=[col_spec(0, False), col_spec(1, False), col_spec(3, False),
                  col_spec(0, True), col_spec(2, True), col_spec(3, True)],
        out_specs=[col_spec(0, False), col_spec(0, True)],
        out_shape=[jax.ShapeDtypeStruct((lay.n_all, d), F32)] * 2,
        scratch_shapes=[state, state],
        compiler_params=_params("arbitrary", "arbitrary"),
        name="hg_scan",
    )(p, p, p, p, p, p)


def _post_mixer(x, y_b, mod_ref, g_ref, wo_ref, win_ref, wout_ref):
    y = jnp.dot(y_b, wo_ref[...], preferred_element_type=F32)
    x = x + _rms(y, mod_ref[5:6, :] * g_ref[3:4, :])
    return _ffn(x, mod_ref, g_ref, win_ref, wout_ref, 1)


def _post_kernel(x_ref, y_ref, mod_ref, g_ref, wo_ref, win_ref, wout_ref, o_ref):
    o_ref[...] = _post_mixer(x_ref[...], y_ref[...], mod_ref, g_ref, wo_ref, win_ref, wout_ref)


def _post_hg_kernel(x_ref, of_ref, ob_ref, gate_ref, hn_ref, mod_ref, g_ref, wo_ref, win_ref,
                    wout_ref, o_ref):
    o = of_ref[...] + ob_ref[...]
    gate = gate_ref[...]
    parts = []
    for h in range(o.shape[1] // HEAD_W):
        sl = slice(h * HEAD_W, (h + 1) * HEAD_W)
        parts.append((_rms(o[:, sl], hn_ref[...]) * _silu(gate[:, sl])).astype(BF16))
    o_ref[...] = _post_mixer(x_ref[...], jnp.concatenate(parts, axis=1), mod_ref, g_ref, wo_ref,
                             win_ref, wout_ref)


def _post_sublayer(lay, kern, x, ys, y_specs, extra, extra_specs, mod_all, norm_g, w_o, w_in, w_out,
                   layer, j, n_tiles, name):
    d = x.shape[1]
    return pl.pallas_call(
        kern,
        grid=(n_tiles,),
        in_specs=[lay.rows(d)] + y_specs + extra_specs
                 + [lay.mod_spec(d, layer), _resident(norm_g.shape, (layer,)),
                    _resident(w_o.shape, (j,)), _resident(w_in.shape, (layer, 1)),
                    _resident(w_out.shape, (layer, 1))],
        out_specs=lay.rows(d),
        out_shape=jax.ShapeDtypeStruct((n_tiles * lay.tm, d), F32),
        compiler_params=_params("parallel"),
        name=name,
    )(x, *ys, *extra, mod_all, norm_g, w_o, w_in, w_out)


def _rope_tables(rows, head_dim, n_id_rows, maps):
    pairs = head_dim // 4
    inv_freq = jnp.power(ROPE_THETA, -jnp.arange(pairs, dtype=F32) / pairs)
    r = jnp.repeat(jnp.arange(rows, dtype=F32), GRID_W)
    col = jnp.tile(jnp.arange(GRID_W, dtype=F32), rows)
    ang = jnp.concatenate([r[:, None] * inv_freq, col[:, None] * inv_freq], axis=-1)
    cos, sin = jnp.cos(ang), jnp.sin(ang)
    cos = jnp.tile(cos, (1, 2 * maps))
    sin = jnp.concatenate([-jnp.tile(sin, (1, maps)), jnp.tile(sin, (1, maps))], axis=-1)
    cos = jnp.concatenate([cos, jnp.ones((n_id_rows, LANES), F32)], axis=0)
    sin = jnp.concatenate([sin, jnp.zeros((n_id_rows, LANES), F32)], axis=0)
    return cos, sin


def _da_head_perm():
    p = np.arange(2)[:, None, None]
    m = np.arange(2)[None, :, None]
    j = np.arange(HEAD_W // 4)[None, None, :]
    return (m * (HEAD_W // 2) + 2 * j + p).reshape(-1)


def _gqa_head_perm():
    p = np.arange(2)[:, None]
    j = np.arange(HEAD_W // 2)[None, :]
    return (2 * j + p).reshape(-1)


def _permute_heads(n_blocks, perm, n_tail):
    idx = (np.arange(n_blocks)[:, None] * HEAD_W + perm[None, :]).reshape(-1)
    return np.concatenate([idx, n_blocks * HEAD_W + np.arange(n_tail)])


def kernel(x, c, ctx, c_ctx, w_mod, b_mod, norm_g, ffn_w_in, ffn_w_out, da_w_qkv, da_lambda, da_subln, da_w_o, hg_w_in, hg_lower_bound, hg_norm, hg_w_o, gqa_w_qkv, gqa_q_norm, gqa_k_norm, gqa_w_o):
    batch, seq, d = x.shape
    n_ctx = ctx.shape[1]
    depth = w_mod.shape[0]
    assert d == N_HEADS * HEAD_W
    lay = _Layout(batch, seq, n_ctx, TOKEN_TILE)
    lay_p = _Layout(batch, seq, n_ctx, PROJ_TILE)
    rows = seq // GRID_W

    pad = (-(batch + 1)) % SUBLANES
    c_rows = jnp.concatenate([c, c_ctx[None, :], jnp.zeros((pad, d), F32)], axis=0)
    mod_all = _modulation(c_rows, w_mod, b_mod)[:, :batch + 1].reshape(depth, batch + 1, N_MOD, d)

    w_in, w_out = ffn_w_in.astype(BF16), ffn_w_out.astype(BF16)
    da_cos, da_sin = _rope_tables(rows, HEAD_W // 2, lay_p.tm, 2)
    gqa_cos, gqa_sin = _rope_tables(rows, HEAD_W, lay_p.tm, 1)
    da_w = da_w_qkv.astype(BF16)[:, :, _permute_heads(2 * N_HEADS, _da_head_perm(), d)]
    gqa_perm = _gqa_head_perm()
    kvw = d // GQA_GROUP
    gqa_w = gqa_w_qkv.astype(BF16)[:, :, _permute_heads(N_HEADS + N_HEADS // GQA_GROUP, gqa_perm, kvw)]
    gqa_qn, gqa_kn = gqa_q_norm[:, None, gqa_perm], gqa_k_norm[:, None, gqa_perm]
    da_wo, hg_wo, gqa_wo = da_w_o.astype(BF16), hg_w_o.astype(BF16), gqa_w_o.astype(BF16)
    hg_w = hg_w_in.astype(BF16)
    lb_table = jnp.cumsum(jax.nn.softmax(hg_lower_bound.astype(F32), axis=1), axis=1)
    lb_table = jnp.swapaxes(lb_table - lb_table[:, :1], 0, 1)
    da_gain, hg_gain = da_subln[:, None, :], hg_norm[:, None, :]

    xs = [x.reshape(batch * seq, d), ctx.reshape(batch * n_ctx, d)]
    for i in range(depth):
        kind, j = i % N_MIXERS, i // N_MIXERS
        need_ctx = i < depth - 1
        xa = _ffn_sublayer(lay, xs, mod_all, norm_g, w_in, w_out, i)
        post = functools.partial(_post_sublayer, mod_all=mod_all, norm_g=norm_g, w_in=w_in,
                                 w_out=w_out, layer=i, j=j)

        if kind == 0:
            lam_init = 0.8 - 0.6 * math.exp(-0.3 * i)
            q, k, v = _project(
                lay_p, functools.partial(_proj_da_kernel, q_scale=LOG2_E * (HEAD_W // 2) ** -0.5),
                xa, mod_all, norm_g, da_w, i, j, [da_cos, da_sin],
                [lay_p.rope_spec(), lay_p.rope_spec()],
                [d, d, d], BF16, "proj_da")
            y = _attention(
                lay, functools.partial(_da_attn_kernel, lam_init=lam_init), q, k, v,
                [da_lambda, da_gain],
                [_resident(da_lambda.shape, (j,)), _resident(da_gain.shape, (j,))],
                kv_heads=N_HEADS, q_per_kv=1, tq=DA_Q_TILE, need_ctx=need_ctx, name="attn_da")
            xa = post(lay, _post_kernel, xa, [y], [lay.rows(d)], [], [], w_o=da_wo,
                      n_tiles=lay.all_tiles if need_ctx else lay.lat_tiles, name="post_da")
        elif kind == 1:
            p, = _project(lay_p, _proj_hg_kernel, xa, mod_all, norm_g, hg_w, i, j,
                          [lb_table], [_resident(lb_table.shape, (i,))], [5 * d], F32, "proj_hg")
            o_f, o_b = _hg_scan(lay, p, d)
            xa = post(lay_p, _post_hg_kernel, xa, [o_f, o_b, p],
                      [lay_p.rows(d), lay_p.rows(d), lay_p.rows(d, 4)],
                      [hg_gain], [_resident(hg_gain.shape, (j,))], w_o=hg_wo,
                      n_tiles=lay_p.all_tiles if need_ctx else lay_p.lat_tiles, name="post_hg")
        else:
            q, k, v = _project(
                lay_p, functools.partial(_proj_gqa_kernel, q_scale=LOG2_E * HEAD_W ** -0.5),
                xa, mod_all, norm_g, gqa_w, i, j,
                [gqa_cos, gqa_sin, gqa_qn, gqa_kn],
                [lay_p.rope_spec(), lay_p.rope_spec(), _resident(gqa_qn.shape, (j,)),
                 _resident(gqa_kn.shape, (j,))],
                [d, kvw, kvw], BF16, "proj_gqa")
            y = _attention(lay, _gqa_attn_kernel, q, k, v, [], [],
                           kv_heads=N_HEADS // GQA_GROUP, q_per_kv=GQA_GROUP,
                           tq=GQA_Q_TILE, need_ctx=need_ctx, name="attn_gqa")
            xa = post(lay, _post_kernel, xa, [y], [lay.rows(d)], [], [], w_o=gqa_wo,
                      n_tiles=lay.all_tiles if need_ctx else lay.lat_tiles, name="post_gqa")
        xs = [xa]
    return xa.reshape(batch, seq, d)
```

```python
import functools
import math

import numpy as np
import jax
import jax.numpy as jnp
from jax import lax
from jax.experimental import pallas as pl
from jax.experimental.pallas import tpu as pltpu

F32 = jnp.float32
BF16 = jnp.bfloat16

NORM_EPS = 1e-6
ROPE_THETA = 10000.0
GRID_W = 64
N_MOD = 9
N_MIXERS = 3
MACARON_WEIGHT = 0.5
LOG2_E = math.log2(math.e)
LANES = 128
SUBLANES = 8
MXU_W = 256
HEAD_W = 128
N_HEADS = 8
GQA_GROUP = 4
VMEM_LIMIT = 56 * 1024 * 1024

TOKEN_TILE = 1024
PROJ_TILE = 512
DA_Q_TILE = 512
GQA_Q_TILE = 256
KEY_BLOCK = 512
ATTN_KV_PER_STEP = 2
SCORE_LOOKAHEAD = 1
HG_CHUNK = 64
HG_CHUNKS_PER_STEP = 4


def _params(*sem):
    return pltpu.CompilerParams(dimension_semantics=sem, vmem_limit_bytes=VMEM_LIMIT)


def _resident(shape, lead=()):
    block = (None,) * len(lead) + tuple(shape[len(lead):])
    index = tuple(lead) + (0,) * (len(shape) - len(lead))
    return pl.BlockSpec(block, lambda *_: index, pipeline_mode=pl.Buffered(1))


def _rms(x, g):
    return x * lax.rsqrt(jnp.mean(x * x, axis=-1, keepdims=True) + NORM_EPS) * g


def _modulated(x, g, shift, scale):
    return _rms(x, g * (1.0 + scale)) + shift


def _silu(x):
    return x * jax.nn.sigmoid(x)


def _mod_kernel(c_ref, w_ref, b_ref, o_ref):
    s = _silu(c_ref[...])
    o_ref[0] = jnp.dot(s, w_ref[0], preferred_element_type=F32,
                       precision=lax.Precision.HIGHEST) + b_ref[0]


def _modulation(c_rows, w_mod, b_mod):
    depth, d, n = w_mod.shape
    rows = c_rows.shape[0]
    tn = n // 8
    return pl.pallas_call(
        _mod_kernel,
        grid=(depth, n // tn),
        in_specs=[pl.BlockSpec((rows, d), lambda i, j: (0, 0)),
                  pl.BlockSpec((1, d, tn), lambda i, j: (i, 0, j)),
                  pl.BlockSpec((1, 1, tn), lambda i, j: (i, 0, j))],
        out_specs=pl.BlockSpec((1, rows, tn), lambda i, j: (i, 0, j)),
        out_shape=jax.ShapeDtypeStruct((depth, rows, n), F32),
        compiler_params=_params("arbitrary", "arbitrary"),
        name="modulation",
    )(c_rows, w_mod, b_mod.reshape(depth, 1, n))


class _Layout:
    def __init__(self, batch, seq, ctx, tm=TOKEN_TILE):
        self.batch, self.seq, self.ctx = batch, seq, ctx
        self.n_lat = batch * seq
        self.n_all = self.n_lat + batch * ctx
        assert seq % tm == 0 and (batch * ctx) % tm == 0
        self.tm = tm
        self.lat_tiles = self.n_lat // tm
        self.all_tiles = self.n_all // tm
        self.tiles_per_sample = seq // tm

    def mod_spec(self, d, layer):
        lat_tiles, tps, batch = self.lat_tiles, self.tiles_per_sample, self.batch
        return pl.BlockSpec((None, None, N_MOD, d),
                            lambda t: (layer, jnp.where(t < lat_tiles, t // tps, batch), 0, 0))

    def rope_spec(self):
        lat_tiles, tps = self.lat_tiles, self.tiles_per_sample
        return pl.BlockSpec((self.tm, LANES),
                            lambda t: (jnp.where(t < lat_tiles, t % tps, tps), 0))

    def rows(self, width, col=0):
        return pl.BlockSpec((self.tm, width), lambda t: (t, col))

    def split_rows(self, width):
        lat_tiles, tm = self.lat_tiles, self.tm
        return [pl.BlockSpec((tm, width), lambda t: (jnp.minimum(t, lat_tiles - 1), 0)),
                pl.BlockSpec((tm, width), lambda t: (jnp.maximum(t - lat_tiles, 0), 0),
                             pipeline_mode=pl.Buffered(1))]


def _ffn_chunks(d_ff):
    assert d_ff % MXU_W == 0
    first = (d_ff // MXU_W + 1) // 2 * MXU_W
    return [(0, first), (first, d_ff - first)]


def _ffn(x, mod_ref, g_ref, win_ref, wout_ref, which):
    mod_base, g_base = 6 * which, 4 * which
    d_ff = wout_ref.shape[0]
    shift = mod_ref[mod_base:mod_base + 1, :]
    scale = mod_ref[mod_base + 1:mod_base + 2, :]
    gate = mod_ref[mod_base + 2:mod_base + 3, :]
    h = _modulated(x, g_ref[g_base:g_base + 1, :], shift, scale).astype(BF16)
    chunks = _ffn_chunks(d_ff)
    hidden = [(jnp.dot(h, win_ref[:, lo:lo + n], preferred_element_type=F32),
               jnp.dot(h, win_ref[:, d_ff + lo:d_ff + lo + n], preferred_element_type=F32))
              for lo, n in chunks]
    y = None
    for (lo, n), (gt, up) in zip(chunks, hidden):
        a = (_silu(gt) * up).astype(BF16)
        part = jnp.dot(a, wout_ref[lo:lo + n, :], preferred_element_type=F32)
        y = part if y is None else y + part
    return x + _rms(y, (MACARON_WEIGHT * gate) * g_ref[g_base + 1:g_base + 2, :])


def _ffn_kernel(x_ref, mod_ref, g_ref, win_ref, wout_ref, o_ref):
    o_ref[...] = _ffn(x_ref[...], mod_ref, g_ref, win_ref, wout_ref, 0)


def _ffn_first_kernel(xl_ref, xc_ref, mod_ref, g_ref, win_ref, wout_ref, o_ref, *, lat_tiles):
    x = jnp.where(pl.program_id(0) < lat_tiles, xl_ref[...], xc_ref[...])
    o_ref[...] = _ffn(x, mod_ref, g_ref, win_ref, wout_ref, 0)


def _ffn_sublayer(lay, xs, mod_all, norm_g, w_in, w_out, layer):
    d = xs[0].shape[1]
    if len(xs) == 1:
        kern, x_specs = _ffn_kernel, [lay.rows(d)]
    else:
        kern, x_specs = functools.partial(_ffn_first_kernel, lat_tiles=lay.lat_tiles), lay.split_rows(d)
    return pl.pallas_call(
        kern,
        grid=(lay.all_tiles,),
        in_specs=x_specs + [lay.mod_spec(d, layer), _resident(norm_g.shape, (layer,)),
                            _resident(w_in.shape, (layer, 0)), _resident(w_out.shape, (layer, 0))],
        out_specs=lay.rows(d),
        out_shape=jax.ShapeDtypeStruct((lay.n_all, d), F32),
        compiler_params=_params("parallel"),
        name="ffn0",
    )(*xs, mod_all, norm_g, w_in, w_out)


def _mixer_matmul(x_ref, mod_ref, g_ref, w_ref):
    h = _modulated(x_ref[...], g_ref[2:3, :], mod_ref[3:4, :], mod_ref[4:5, :]).astype(BF16)
    return jnp.dot(h, w_ref[...], preferred_element_type=F32)


def _rope(x, cos, sin):
    return x * cos + pltpu.roll(x, LANES // 2, 1) * sin


def _proj_da_kernel(x_ref, mod_ref, g_ref, w_ref, cos_ref, sin_ref, q_ref, k_ref, v_ref, *, q_scale):
    qkv = _mixer_matmul(x_ref, mod_ref, g_ref, w_ref)
    cos, sin = cos_ref[...], sin_ref[...]
    width = q_ref.shape[1]
    cos_q, sin_q = cos * q_scale, sin * q_scale
    for lo in range(0, width, HEAD_W):
        q_ref[:, lo:lo + HEAD_W] = _rope(qkv[:, lo:lo + HEAD_W], cos_q, sin_q).astype(BF16)
        k_ref[:, lo:lo + HEAD_W] = _rope(qkv[:, width + lo:width + lo + HEAD_W], cos, sin).astype(BF16)
    v_ref[...] = qkv[:, 2 * width:].astype(BF16)


def _head_rms(x, g, ones):
    ss = jnp.dot((x * x).astype(BF16), ones, preferred_element_type=F32)
    return x * lax.rsqrt(ss * (1.0 / HEAD_W) + NORM_EPS) * g


def _proj_gqa_kernel(x_ref, mod_ref, g_ref, w_ref, cos_ref, sin_ref, qn_ref, kn_ref,
                     q_ref, k_ref, v_ref, *, q_scale):
    qkv = _mixer_matmul(x_ref, mod_ref, g_ref, w_ref)
    cos, sin = cos_ref[...], sin_ref[...]
    qw, kw = q_ref.shape[1], k_ref.shape[1]
    ones = jnp.ones((HEAD_W, HEAD_W), BF16)
    q_gain = qn_ref[...] * q_scale
    for lo in range(0, qw, HEAD_W):
        xq = _head_rms(qkv[:, lo:lo + HEAD_W], q_gain, ones)
        q_ref[:, lo:lo + HEAD_W] = _rope(xq, cos, sin).astype(BF16)
    for lo in range(0, kw, HEAD_W):
        xk = _head_rms(qkv[:, qw + lo:qw + lo + HEAD_W], kn_ref[...], ones)
        k_ref[:, lo:lo + HEAD_W] = _rope(xk, cos, sin).astype(BF16)
    v_ref[...] = qkv[:, qw + kw:].astype(BF16)


def _proj_hg_kernel(x_ref, mod_ref, g_ref, w_ref, lb_ref, o_ref):
    p = _mixer_matmul(x_ref, mod_ref, g_ref, w_ref)
    d = x_ref.shape[1]
    o_ref[:, 0:d] = _silu(p[:, 0:d])
    for k in range(2):
        lb = lb_ref[k:k + 1, :]
        f = lb + (1.0 - lb) * jax.nn.sigmoid(p[:, (1 + k) * d:(2 + k) * d])
        o_ref[:, (1 + k) * d:(2 + k) * d] = jnp.log2(f)
    o_ref[:, 3 * d:] = p[:, 3 * d:]


def _project(lay, kern, x, mod_all, norm_g, w, layer, j, extra, extra_specs, out_widths, out_dtype,
             name):
    d = x.shape[1]
    outs = pl.pallas_call(
        kern,
        grid=(lay.all_tiles,),
        in_specs=[lay.rows(d), lay.mod_spec(d, layer), _resident(norm_g.shape, (layer,)),
                  _resident(w.shape, (j,))] + extra_specs,
        out_specs=[lay.rows(wd) for wd in out_widths],
        out_shape=[jax.ShapeDtypeStruct((lay.n_all, wd), out_dtype) for wd in out_widths],
        compiler_params=_params("parallel"),
        name=name,
    )(x, mod_all, norm_g, w, *extra)
    return outs


def _nt_dot(a, b):
    return lax.dot_general(a, b, (((1,), (1,)), ((), ())), preferred_element_type=F32)


def _stage_keys(k_refs, v_refs, k_s, v_s):
    for h in range(k_s.shape[0]):
        sl = slice(h * HEAD_W, (h + 1) * HEAD_W)
        row = 0
        for k_ref, v_ref in zip(k_refs, v_refs):
            n = k_ref.shape[0]
            k_s[h, row:row + n, :] = k_ref[:, sl]
            v_s[h, row:row + n, 0:HEAD_W] = v_ref[:, sl]
            row += n
        v_s[h, :, HEAD_W:] = jnp.ones((v_s.shape[1], HEAD_W), BF16)


def _key_blocks(start, stop):
    return [(s, min(KEY_BLOCK, stop - s)) for s in range(start, stop, KEY_BLOCK)]


def _softmax_pv(queries, k_s, v_s, blocks, finish):
    steps = [(h, blk) for h in range(len(queries)) for blk in blocks]

    def scores(step):
        h, (start, size) = step
        return _nt_dot(queries[h], k_s[h, start:start + size, :])

    m = acc = None
    ahead = [scores(step) for step in steps[:SCORE_LOOKAHEAD]]
    for n, (h, (start, size)) in enumerate(steps):
        s = ahead.pop(0)
        if n + SCORE_LOOKAHEAD < len(steps):
            ahead.append(scores(steps[n + SCORE_LOOKAHEAD]))
        m_new = jnp.max(s, axis=-1, keepdims=True)
        if m is not None:
            m_new = jnp.maximum(m, m_new)
        e = jnp.exp2(s - m_new).astype(BF16)
        pv = jnp.dot(e, v_s[h, start:start + size, :], preferred_element_type=F32)
        acc = pv if acc is None else acc * jnp.exp2(m - m_new) + pv
        m = m_new
        if (start, size) == blocks[-1]:
            finish(h, acc[:, :HEAD_W] / acc[:, HEAD_W:])
            m = acc = None


def _da_attend(q_ref, k_s, v_s, blocks, lam, gain, o_ref):
    tq = q_ref.shape[0]
    lane = lax.broadcasted_iota(jnp.int32, (1, LANES), 1)
    first_map = (lane % (LANES // 2)) < (LANES // 4)
    queries = []
    for h in range(k_s.shape[0]):
        q = q_ref[:, h * HEAD_W:(h + 1) * HEAD_W]
        zero = jnp.zeros_like(q)
        queries.append(jnp.concatenate([jnp.where(first_map, q, zero),
                                        jnp.where(first_map, zero, q)], axis=0))

    def finish(h, p):
        o_ref[:, h * HEAD_W:(h + 1) * HEAD_W] = _rms(p[:tq] - lam * p[tq:], gain).astype(BF16)

    _softmax_pv(queries, k_s, v_s, blocks, finish)


def _split_refs(refs, n_seg, n_extra):
    keys, values = refs[1:1 + n_seg], refs[1 + n_seg:1 + 2 * n_seg]
    extra = refs[1 + 2 * n_seg:1 + 2 * n_seg + n_extra]
    return (refs[0], keys, values, extra) + tuple(refs[-3:])


def _da_attn_kernel(*refs, n_seg, lam_init):
    q_ref, k_refs, v_refs, (lam_ref, g_ref), o_ref, k_s, v_s = _split_refs(refs, n_seg, 2)
    lp = lam_ref[...]
    lam = (jnp.exp(jnp.sum(lp[0:1] * lp[1:2], axis=-1, keepdims=True))
           - jnp.exp(jnp.sum(lp[2:3] * lp[3:4], axis=-1, keepdims=True)) + lam_init)
    gain = g_ref[...] * (1.0 - lam_init)

    @pl.when(pl.program_id(2) == 0)
    def _():
        _stage_keys(k_refs, v_refs, k_s, v_s)

    _da_attend(q_ref, k_s, v_s, _key_blocks(0, k_s.shape[1]), lam, gain, o_ref)


def _gqa_attend(q_ref, k_s, v_s, blocks, o_ref):
    tq = q_ref.shape[0]
    group_w = GQA_GROUP * HEAD_W
    queries = [jnp.concatenate([q_ref[:, h * group_w + g * HEAD_W:h * group_w + (g + 1) * HEAD_W]
                                for g in range(GQA_GROUP)], axis=0) for h in range(k_s.shape[0])]

    def finish(h, o):
        for g in range(GQA_GROUP):
            lo = h * group_w + g * HEAD_W
            o_ref[:, lo:lo + HEAD_W] = o[g * tq:(g + 1) * tq].astype(BF16)

    _softmax_pv(queries, k_s, v_s, blocks, finish)


def _gqa_attn_kernel(*refs, n_seg):
    q_ref, k_refs, v_refs, _, o_ref, k_s, v_s = _split_refs(refs, n_seg, 0)

    @pl.when(pl.program_id(2) == 0)
    def _():
        _stage_keys(k_refs, v_refs, k_s, v_s)

    _gqa_attend(q_ref, k_s, v_s, _key_blocks(0, k_s.shape[1]), o_ref)


def _attention(lay, kern, q, k, v, extra, extra_specs, *, kv_heads, q_per_kv, tq, need_ctx, name):
    batch, seq, ctx, n_lat = lay.batch, lay.seq, lay.ctx, lay.n_lat
    hp = ATTN_KV_PER_STEP
    tc = min(tq, ctx)
    assert seq % tq == 0 and ctx % tc == 0 and n_lat % ctx == 0 and kv_heads % hp == 0
    q_block_w, kv_block_w = hp * q_per_kv * HEAD_W, hp * HEAD_W
    kv_lat = pl.BlockSpec((seq, kv_block_w), lambda b, h, i: (b, h))
    kv_ctx = pl.BlockSpec((ctx, kv_block_w), lambda b, h, i: (n_lat // ctx + b, h))

    def call(tile, tiles, first_tile, kv_specs, n_keys, call_name):
        n_seg = len(kv_specs)
        return pl.pallas_call(
            functools.partial(kern, n_seg=n_seg),
            grid=(batch, kv_heads // hp, tiles),
            in_specs=[pl.BlockSpec((tile, q_block_w), lambda b, h, i: (first_tile + b * tiles + i, h))]
                     + kv_specs + kv_specs + extra_specs,
            out_specs=pl.BlockSpec((tile, q_block_w), lambda b, h, i: (b * tiles + i, h)),
            out_shape=jax.ShapeDtypeStruct((batch * tiles * tile, q.shape[1]), BF16),
            scratch_shapes=[pltpu.VMEM((hp, n_keys, HEAD_W), BF16),
                            pltpu.VMEM((hp, n_keys, 2 * HEAD_W), BF16)],
            compiler_params=_params("parallel", "parallel", "arbitrary"),
            name=call_name,
        )(q, *[k] * n_seg, *[v] * n_seg, *extra)

    y_lat = call(tq, seq // tq, 0, [kv_lat, kv_ctx], seq + ctx, name)
    y_ctx = call(tc, ctx // tc, n_lat // tc, [kv_ctx], ctx, name + "_ctx") if need_ctx else None
    return y_lat, y_ctx


def _hg_direction(c, reverse):
    row = lax.broadcasted_iota(jnp.int32, (c, c), 0)
    col = lax.broadcasted_iota(jnp.int32, (c, c), 1)
    later, earlier = (col, row) if reverse else (row, col)
    tri = (earlier <= later).astype(F32)

    levels = []
    half = c // 2
    while half >= 1:
        same = (row // (2 * half)) == (col // (2 * half))
        q_side = (later % (2 * half)) >= half
        k_side = (earlier % (2 * half)) < half
        levels.append((half, same & q_side & k_side))
        half //= 2
    nv = c // SUBLANES
    t_loc = lax.broadcasted_iota(jnp.int32, (nv, SUBLANES, HEAD_W), 1)

    def boundary(cum, half):
        b_row = half if reverse else half - 1
        if 2 * half >= SUBLANES:
            c3 = cum.reshape(c // (2 * half), 2 * half, HEAD_W)
            return jnp.broadcast_to(c3[:, b_row:b_row + 1, :], c3.shape).reshape(c, HEAD_W)
        c3 = cum.reshape(nv, SUBLANES, HEAD_W)
        ref = None
        for blk in range(SUBLANES // (2 * half)):
            r = blk * 2 * half + b_row
            cand = jnp.broadcast_to(c3[:, r:r + 1, :], c3.shape)
            ref = cand if ref is None else jnp.where(t_loc >= blk * 2 * half, cand, ref)
        return ref.reshape(c, HEAD_W)

    return tri, row == col, levels, boundary, (0 if reverse else c - 1)


def _hg_scan_kernel(qf_ref, lff_ref, vf_ref, qb_ref, lfb_ref, vb_ref, of_ref, ob_ref, stf_ref, stb_ref):
    c = HG_CHUNK
    n_sub = qf_ref.shape[0] // c

    @pl.when(pl.program_id(1) == 0)
    def _():
        stf_ref[...] = jnp.zeros_like(stf_ref)
        stb_ref[...] = jnp.zeros_like(stb_ref)

    fwd, bwd = _hg_direction(c, False), _hg_direction(c, True)
    heads = range(qf_ref.shape[1] // HEAD_W)
    states_f = [stf_ref[h] for h in heads]
    states_b = [stb_ref[h] for h in heads]

    for sub in range(n_sub):
        rows = slice(sub * c, (sub + 1) * c)
        _hg_chunk(qf_ref, lff_ref, vf_ref, of_ref, rows, states_f, *fwd)
        rows = slice((n_sub - 1 - sub) * c, (n_sub - sub) * c)
        _hg_chunk(qb_ref, lfb_ref, vb_ref, ob_ref, rows, states_b, *bwd)

    for h in heads:
        stf_ref[h] = states_f[h]
        stb_ref[h] = states_b[h]


def _hg_chunk(q_ref, lf_ref, v_ref, o_ref, rows, states, tri, diagonal, levels, boundary, end):
    cum_all = jnp.dot(tri, lf_ref[rows, :], preferred_element_type=F32,
                      precision=lax.Precision.HIGHEST)
    staged = []
    for h in range(len(states)):
        sl = slice(h * HEAD_W, (h + 1) * HEAD_W)
        q, lf, v = q_ref[rows, sl], lf_ref[rows, sl], v_ref[rows, sl]
        cum = cum_all[:, sl]
        f = jnp.exp2(lf)
        k = 1.0 - f
        v_b, k_b = v.astype(BF16), k.astype(BF16)
        att = jnp.where(diagonal, _nt_dot(q.astype(BF16), k_b), 0.0)
        for half, mask in levels:
            if half == 1:
                qt, kt = (q * f).astype(BF16), k_b
            else:
                decay = jnp.exp2(-jnp.abs(cum - boundary(cum, half)))
                qt, kt = (q * decay).astype(BF16), (k * decay).astype(BF16)
            att = jnp.where(mask, _nt_dot(qt, kt), att)
        total = cum[end:end + 1, :]
        st = states[h]
        inter = _nt_dot((q * jnp.exp2(cum)).astype(BF16), st.astype(BF16))
        kd = (k * jnp.exp2(total - cum)).astype(BF16)
        upd = lax.dot_general(v_b, kd, (((0,), (0,)), ((), ())), preferred_element_type=F32)
        states[h] = st * jnp.exp2(total) + upd
        staged.append((sl, att.astype(BF16), v_b, inter))

    for sl, att_b, v_b, inter in staged:
        o_ref[rows, sl] = jnp.dot(att_b, v_b, preferred_element_type=F32) + inter


def _hg_scan(lay, p, d):
    batch, seq, ctx, n_lat = lay.batch, lay.seq, lay.ctx, lay.n_lat
    c = HG_CHUNK * HG_CHUNKS_PER_STEP
    assert seq % c == 0 and ctx % c == 0
    nlc, ncc = seq // c, ctx // c
    ctx_base = n_lat // c

    def blk(b, j, reverse):
        c_ctx = (ncc - 1 - j) if reverse else j
        c_lat = (nlc - 1 - (j - ncc)) if reverse else (j - ncc)
        return jnp.where(j < ncc, ctx_base + b * ncc + c_ctx, b * nlc + c_lat)

    def col_spec(colblk, reverse):
        return pl.BlockSpec((c, d), lambda b, j: (blk(b, j, reverse), colblk))

    state = pltpu.VMEM((d // HEAD_W, HEAD_W, HEAD_W), F32)
    return pl.pallas_call(
        _hg_scan_kernel,
        grid=(batch, ncc + nlc),
        in_specs=[col_spec(0, False), col_spec(1, False), col_spec(3, False),
                  col_spec(0, True), col_spec(2, True), col_spec(3, True)],
        out_specs=[col_spec(0, False), col_spec(0, True)],
        out_shape=[jax.ShapeDtypeStruct((lay.n_all, d), F32)] * 2,
        scratch_shapes=[state, state],
        compiler_params=_params("arbitrary", "arbitrary"),
        name="hg_scan",
    )(p, p, p, p, p, p)


def _post_mixer(x, y_b, mod_ref, g_ref, wo_ref, win_ref, wout_ref):
    y = jnp.dot(y_b, wo_ref[...], preferred_element_type=F32)
    x = x + _rms(y, mod_ref[5:6, :] * g_ref[3:4, :])
    return _ffn(x, mod_ref, g_ref, win_ref, wout_ref, 1)


def _post_kernel(x_ref, y_ref, mod_ref, g_ref, wo_ref, win_ref, wout_ref, o_ref):
    o_ref[...] = _post_mixer(x_ref[...], y_ref[...], mod_ref, g_ref, wo_ref, win_ref, wout_ref)


def _post_split_kernel(x_ref, yl_ref, yc_ref, mod_ref, g_ref, wo_ref, win_ref, wout_ref, o_ref,
                       *, lat_tiles):
    y = jnp.where(pl.program_id(0) < lat_tiles, yl_ref[...], yc_ref[...])
    o_ref[...] = _post_mixer(x_ref[...], y, mod_ref, g_ref, wo_ref, win_ref, wout_ref)


def _post_hg_kernel(x_ref, of_ref, ob_ref, gate_ref, hn_ref, mod_ref, g_ref, wo_ref, win_ref,
                    wout_ref, o_ref):
    o = of_ref[...] + ob_ref[...]
    gate = gate_ref[...]
    parts = []
    for h in range(o.shape[1] // HEAD_W):
        sl = slice(h * HEAD_W, (h + 1) * HEAD_W)
        parts.append((_rms(o[:, sl], hn_ref[...]) * _silu(gate[:, sl])).astype(BF16))
    o_ref[...] = _post_mixer(x_ref[...], jnp.concatenate(parts, axis=1), mod_ref, g_ref, wo_ref,
                             win_ref, wout_ref)


def _post_sublayer(lay, kern, x, ys, y_specs, extra, extra_specs, mod_all, norm_g, w_o, w_in, w_out,
                   layer, j, n_tiles, name):
    d = x.shape[1]
    return pl.pallas_call(
        kern,
        grid=(n_tiles,),
        in_specs=[lay.rows(d)] + y_specs + extra_specs
                 + [lay.mod_spec(d, layer), _resident(norm_g.shape, (layer,)),
                    _resident(w_o.shape, (j,)), _resident(w_in.shape, (layer, 1)),
                    _resident(w_out.shape, (layer, 1))],
        out_specs=lay.rows(d),
        out_shape=jax.ShapeDtypeStruct((n_tiles * lay.tm, d), F32),
        compiler_params=_params("parallel"),
        name=name,
    )(x, *ys, *extra, mod_all, norm_g, w_o, w_in, w_out)


def _rope_tables(rows, head_dim, n_id_rows, maps):
    pairs = head_dim // 4
    inv_freq = jnp.power(ROPE_THETA, -jnp.arange(pairs, dtype=F32) / pairs)
    r = jnp.repeat(jnp.arange(rows, dtype=F32), GRID_W)
    col = jnp.tile(jnp.arange(GRID_W, dtype=F32), rows)
    ang = jnp.concatenate([r[:, None] * inv_freq, col[:, None] * inv_freq], axis=-1)
    cos, sin = jnp.cos(ang), jnp.sin(ang)
    cos = jnp.tile(cos, (1, 2 * maps))
    sin = jnp.concatenate([-jnp.tile(sin, (1, maps)), jnp.tile(sin, (1, maps))], axis=-1)
    cos = jnp.concatenate([cos, jnp.ones((n_id_rows, LANES), F32)], axis=0)
    sin = jnp.concatenate([sin, jnp.zeros((n_id_rows, LANES), F32)], axis=0)
    return cos, sin


def _da_head_perm():
    p = np.arange(2)[:, None, None]
    m = np.arange(2)[None, :, None]
    j = np.arange(HEAD_W // 4)[None, None, :]
    return (m * (HEAD_W // 2) + 2 * j + p).reshape(-1)


def _gqa_head_perm():
    p = np.arange(2)[:, None]
    j = np.arange(HEAD_W // 2)[None, :]
    return (2 * j + p).reshape(-1)


def _permute_heads(n_blocks, perm, n_tail):
    idx = (np.arange(n_blocks)[:, None] * HEAD_W + perm[None, :]).reshape(-1)
    return np.concatenate([idx, n_blocks * HEAD_W + np.arange(n_tail)])


def kernel(x, c, ctx, c_ctx, w_mod, b_mod, norm_g, ffn_w_in, ffn_w_out, da_w_qkv, da_lambda, da_subln, da_w_o, hg_w_in, hg_lower_bound, hg_norm, hg_w_o, gqa_w_qkv, gqa_q_norm, gqa_k_norm, gqa_w_o):
    batch, seq, d = x.shape
    n_ctx = ctx.shape[1]
    depth = w_mod.shape[0]
    assert d == N_HEADS * HEAD_W
    lay = _Layout(batch, seq, n_ctx, TOKEN_TILE)
    lay_p = _Layout(batch, seq, n_ctx, PROJ_TILE)
    rows = seq // GRID_W

    pad = (-(batch + 1)) % SUBLANES
    c_rows = jnp.concatenate([c, c_ctx[None, :], jnp.zeros((pad, d), F32)], axis=0)
    mod_all = _modulation(c_rows, w_mod, b_mod)[:, :batch + 1].reshape(depth, batch + 1, N_MOD, d)

    w_in, w_out = ffn_w_in.astype(BF16), ffn_w_out.astype(BF16)
    da_cos, da_sin = _rope_tables(rows, HEAD_W // 2, lay_p.tm, 2)
    gqa_cos, gqa_sin = _rope_tables(rows, HEAD_W, lay_p.tm, 1)
    da_w = da_w_qkv.astype(BF16)[:, :, _permute_heads(2 * N_HEADS, _da_head_perm(), d)]
    gqa_perm = _gqa_head_perm()
    kvw = d // GQA_GROUP
    gqa_w = gqa_w_qkv.astype(BF16)[:, :, _permute_heads(N_HEADS + N_HEADS // GQA_GROUP, gqa_perm, kvw)]
    gqa_qn, gqa_kn = gqa_q_norm[:, None, gqa_perm], gqa_k_norm[:, None, gqa_perm]
    da_wo, hg_wo, gqa_wo = da_w_o.astype(BF16), hg_w_o.astype(BF16), gqa_w_o.astype(BF16)
    hg_w = hg_w_in.astype(BF16)
    lb_table = jnp.cumsum(jax.nn.softmax(hg_lower_bound.astype(F32), axis=1), axis=1)
    lb_table = jnp.swapaxes(lb_table - lb_table[:, :1], 0, 1)
    da_gain, hg_gain = da_subln[:, None, :], hg_norm[:, None, :]

    xs = [x.reshape(batch * seq, d), ctx.reshape(batch * n_ctx, d)]
    for i in range(depth):
        kind, j = i % N_MIXERS, i // N_MIXERS
        need_ctx = i < depth - 1
        xa = _ffn_sublayer(lay, xs, mod_all, norm_g, w_in, w_out, i)
        post = functools.partial(_post_sublayer, mod_all=mod_all, norm_g=norm_g, w_in=w_in,
                                 w_out=w_out, layer=i, j=j)

        def attention_post(ys, w_o, name):
            y_lat, y_ctx = ys
            if y_ctx is None:
                return post(lay, _post_kernel, xa, [y_lat], [lay.rows(d)], [], [], w_o=w_o,
                            n_tiles=lay.lat_tiles, name=name)
            return post(lay, functools.partial(_post_split_kernel, lat_tiles=lay.lat_tiles), xa,
                        [y_lat, y_ctx], lay.split_rows(d), [], [], w_o=w_o,
                        n_tiles=lay.all_tiles, name=name)

        if kind == 0:
            lam_init = 0.8 - 0.6 * math.exp(-0.3 * i)
            q, k, v = _project(
                lay_p, functools.partial(_proj_da_kernel, q_scale=LOG2_E * (HEAD_W // 2) ** -0.5),
                xa, mod_all, norm_g, da_w, i, j, [da_cos, da_sin],
                [lay_p.rope_spec(), lay_p.rope_spec()],
                [d, d, d], BF16, "proj_da")
            ys = _attention(
                lay, functools.partial(_da_attn_kernel, lam_init=lam_init), q, k, v,
                [da_lambda, da_gain],
                [_resident(da_lambda.shape, (j,)), _resident(da_gain.shape, (j,))],
                kv_heads=N_HEADS, q_per_kv=1, tq=DA_Q_TILE, need_ctx=need_ctx, name="attn_da")
            xa = attention_post(ys, da_wo, "post_da")
        elif kind == 1:
            p, = _project(lay_p, _proj_hg_kernel, xa, mod_all, norm_g, hg_w, i, j,
                          [lb_table], [_resident(lb_table.shape, (i,))], [5 * d], F32, "proj_hg")
            o_f, o_b = _hg_scan(lay, p, d)
            xa = post(lay_p, _post_hg_kernel, xa, [o_f, o_b, p],
                      [lay_p.rows(d), lay_p.rows(d), lay_p.rows(d, 4)],
                      [hg_gain], [_resident(hg_gain.shape, (j,))], w_o=hg_wo,
                      n_tiles=lay_p.all_tiles if need_ctx else lay_p.lat_tiles, name="post_hg")
        else:
            q, k, v = _project(
                lay_p, functools.partial(_proj_gqa_kernel, q_scale=LOG2_E * HEAD_W ** -0.5),
                xa, mod_all, norm_g, gqa_w, i, j,
                [gqa_cos, gqa_sin, gqa_qn, gqa_kn],
                [lay_p.rope_spec(), lay_p.rope_spec(), _resident(gqa_qn.shape, (j,)),
                 _resident(gqa_kn.shape, (j,))],
                [d, kvw, kvw], BF16, "proj_gqa")
            ys = _attention(lay, _gqa_attn_kernel, q, k, v, [], [],
                            kv_heads=N_HEADS // GQA_GROUP, q_per_kv=GQA_GROUP,
                            tq=GQA_Q_TILE, need_ctx=need_ctx, name="attn_gqa")
            xa = attention_post(ys, gqa_wo, "post_gqa")
        xs = [xa]
    return xa.reshape(batch, seq, d)
```

```python
import functools
import math

import numpy as np
import jax
import jax.numpy as jnp
from jax import lax
from jax.experimental import pallas as pl
from jax.experimental.pallas import tpu as pltpu

F32 = jnp.float32
BF16 = jnp.bfloat16

NORM_EPS = 1e-6
ROPE_THETA = 10000.0
GRID_W = 64
N_MOD = 9
N_MIXERS = 3
MACARON_WEIGHT = 0.5
LOG2_E = math.log2(math.e)
LANES = 128
SUBLANES = 8
MXU_W = 256
HEAD_W = 128
N_HEADS = 8
GQA_GROUP = 4
VMEM_LIMIT = 56 * 1024 * 1024

TOKEN_TILE = 1024
PROJ_TILE = 512
DA_Q_TILE = 512
GQA_Q_TILE = 256
KEY_BLOCK = 512
ATTN_KV_PER_STEP = 2
SCORE_LOOKAHEAD = 1
HG_CHUNK = 64
HG_CHUNKS_PER_STEP = 4


def _params(*sem):
    return pltpu.CompilerParams(dimension_semantics=sem, vmem_limit_bytes=VMEM_LIMIT)


def _resident(shape, lead=()):
    block = (None,) * len(lead) + tuple(shape[len(lead):])
    index = tuple(lead) + (0,) * (len(shape) - len(lead))
    return pl.BlockSpec(block, lambda *_: index, pipeline_mode=pl.Buffered(1))


def _rms(x, g):
    return x * lax.rsqrt(jnp.mean(x * x, axis=-1, keepdims=True) + NORM_EPS) * g


def _modulated(x, g, shift, scale):
    return _rms(x, g * (1.0 + scale)) + shift


def _silu(x):
    return x * jax.nn.sigmoid(x)


def _mod_kernel(c_ref, w_ref, b_ref, o_ref):
    s = _silu(c_ref[...]).astype(BF16)
    o_ref[0] = jnp.dot(s, w_ref[0].astype(BF16), preferred_element_type=F32) + b_ref[0]


def _modulation(c_rows, w_mod, b_mod):
    depth, d, n = w_mod.shape
    rows = c_rows.shape[0]
    tn = n // 8
    return pl.pallas_call(
        _mod_kernel,
        grid=(depth, n // tn),
        in_specs=[pl.BlockSpec((rows, d), lambda i, j: (0, 0)),
                  pl.BlockSpec((1, d, tn), lambda i, j: (i, 0, j)),
                  pl.BlockSpec((1, 1, tn), lambda i, j: (i, 0, j))],
        out_specs=pl.BlockSpec((1, rows, tn), lambda i, j: (i, 0, j)),
        out_shape=jax.ShapeDtypeStruct((depth, rows, n), F32),
        compiler_params=_params("arbitrary", "arbitrary"),
        name="modulation",
    )(c_rows, w_mod, b_mod.reshape(depth, 1, n))


class _Layout:
    def __init__(self, batch, seq, ctx, tm=TOKEN_TILE):
        self.batch, self.seq, self.ctx = batch, seq, ctx
        self.n_lat = batch * seq
        self.n_all = self.n_lat + batch * ctx
        assert seq % tm == 0 and (batch * ctx) % tm == 0
        self.tm = tm
        self.lat_tiles = self.n_lat // tm
        self.all_tiles = self.n_all // tm
        self.tiles_per_sample = seq // tm

    def mod_spec(self, d, layer):
        lat_tiles, tps, batch = self.lat_tiles, self.tiles_per_sample, self.batch
        return pl.BlockSpec((None, None, N_MOD, d),
                            lambda t: (layer, jnp.where(t < lat_tiles, t // tps, batch), 0, 0))

    def rope_spec(self):
        lat_tiles, tps = self.lat_tiles, self.tiles_per_sample
        return pl.BlockSpec((self.tm, LANES),
                            lambda t: (jnp.where(t < lat_tiles, t % tps, tps), 0))

    def rows(self, width, col=0):
        return pl.BlockSpec((self.tm, width), lambda t: (t, col))

    def split_rows(self, width):
        lat_tiles, tm = self.lat_tiles, self.tm
        return [pl.BlockSpec((tm, width), lambda t: (jnp.minimum(t, lat_tiles - 1), 0)),
                pl.BlockSpec((tm, width), lambda t: (jnp.maximum(t - lat_tiles, 0), 0),
                             pipeline_mode=pl.Buffered(1))]


def _ffn_chunks(d_ff):
    assert d_ff % MXU_W == 0
    first = (d_ff // MXU_W + 1) // 2 * MXU_W
    return [(0, first), (first, d_ff - first)]


def _ffn(x, mod_ref, g_ref, win_ref, wout_ref, which):
    mod_base, g_base = 6 * which, 4 * which
    d_ff = wout_ref.shape[0]
    shift = mod_ref[mod_base:mod_base + 1, :]
    scale = mod_ref[mod_base + 1:mod_base + 2, :]
    gate = mod_ref[mod_base + 2:mod_base + 3, :]
    h = _modulated(x, g_ref[g_base:g_base + 1, :], shift, scale).astype(BF16)
    chunks = _ffn_chunks(d_ff)
    hidden = [(jnp.dot(h, win_ref[:, lo:lo + n], preferred_element_type=F32),
               jnp.dot(h, win_ref[:, d_ff + lo:d_ff + lo + n], preferred_element_type=F32))
              for lo, n in chunks]
    y = None
    for (lo, n), (gt, up) in zip(chunks, hidden):
        a = (_silu(gt) * up).astype(BF16)
        part = jnp.dot(a, wout_ref[lo:lo + n, :], preferred_element_type=F32)
        y = part if y is None else y + part
    return x + _rms(y, (MACARON_WEIGHT * gate) * g_ref[g_base + 1:g_base + 2, :])


def _ffn_kernel(x_ref, mod_ref, g_ref, win_ref, wout_ref, o_ref):
    o_ref[...] = _ffn(x_ref[...], mod_ref, g_ref, win_ref, wout_ref, 0)


def _ffn_first_kernel(xl_ref, xc_ref, mod_ref, g_ref, win_ref, wout_ref, o_ref, *, lat_tiles):
    x = jnp.where(pl.program_id(0) < lat_tiles, xl_ref[...], xc_ref[...])
    o_ref[...] = _ffn(x, mod_ref, g_ref, win_ref, wout_ref, 0)


def _ffn_sublayer(lay, xs, mod_all, norm_g, w_in, w_out, layer):
    d = xs[0].shape[1]
    if len(xs) == 1:
        kern, x_specs = _ffn_kernel, [lay.rows(d)]
    else:
        kern, x_specs = functools.partial(_ffn_first_kernel, lat_tiles=lay.lat_tiles), lay.split_rows(d)
    return pl.pallas_call(
        kern,
        grid=(lay.all_tiles,),
        in_specs=x_specs + [lay.mod_spec(d, layer), _resident(norm_g.shape, (layer,)),
                            _resident(w_in.shape, (layer, 0)), _resident(w_out.shape, (layer, 0))],
        out_specs=lay.rows(d),
        out_shape=jax.ShapeDtypeStruct((lay.n_all, d), F32),
        compiler_params=_params("parallel"),
        name="ffn0",
    )(*xs, mod_all, norm_g, w_in, w_out)


def _mixer_matmul(x_ref, mod_ref, g_ref, w_ref):
    h = _modulated(x_ref[...], g_ref[2:3, :], mod_ref[3:4, :], mod_ref[4:5, :]).astype(BF16)
    return jnp.dot(h, w_ref[...], preferred_element_type=F32)


def _rope(x, cos, sin):
    return x * cos + pltpu.roll(x, LANES // 2, 1) * sin


def _proj_da_kernel(x_ref, mod_ref, g_ref, w_ref, cos_ref, sin_ref, q_ref, k_ref, v_ref, *, q_scale):
    qkv = _mixer_matmul(x_ref, mod_ref, g_ref, w_ref)
    cos, sin = cos_ref[...], sin_ref[...]
    width = q_ref.shape[1]
    cos_q, sin_q = cos * q_scale, sin * q_scale
    for lo in range(0, width, HEAD_W):
        q_ref[:, lo:lo + HEAD_W] = _rope(qkv[:, lo:lo + HEAD_W], cos_q, sin_q).astype(BF16)
        k_ref[:, lo:lo + HEAD_W] = _rope(qkv[:, width + lo:width + lo + HEAD_W], cos, sin).astype(BF16)
    v_ref[...] = qkv[:, 2 * width:].astype(BF16)


def _head_rms(x, g, ones):
    ss = jnp.dot((x * x).astype(BF16), ones, preferred_element_type=F32)
    return x * lax.rsqrt(ss * (1.0 / HEAD_W) + NORM_EPS) * g


def _proj_gqa_kernel(x_ref, mod_ref, g_ref, w_ref, cos_ref, sin_ref, qn_ref, kn_ref,
                     q_ref, k_ref, v_ref, *, q_scale):
    qkv = _mixer_matmul(x_ref, mod_ref, g_ref, w_ref)
    cos, sin = cos_ref[...], sin_ref[...]
    qw, kw = q_ref.shape[1], k_ref.shape[1]
    ones = jnp.ones((HEAD_W, HEAD_W), BF16)
    q_gain = qn_ref[...] * q_scale
    for lo in range(0, qw, HEAD_W):
        xq = _head_rms(qkv[:, lo:lo + HEAD_W], q_gain, ones)
        q_ref[:, lo:lo + HEAD_W] = _rope(xq, cos, sin).astype(BF16)
    for lo in range(0, kw, HEAD_W):
        xk = _head_rms(qkv[:, qw + lo:qw + lo + HEAD_W], kn_ref[...], ones)
        k_ref[:, lo:lo + HEAD_W] = _rope(xk, cos, sin).astype(BF16)
    v_ref[...] = qkv[:, qw + kw:].astype(BF16)


def _proj_hg_kernel(x_ref, mod_ref, g_ref, w_ref, lb_ref, o_ref):
    p = _mixer_matmul(x_ref, mod_ref, g_ref, w_ref)
    d = x_ref.shape[1]
    o_ref[:, 0:d] = _silu(p[:, 0:d])
    for k in range(2):
        lb = lb_ref[k:k + 1, :]
        f = lb + (1.0 - lb) * jax.nn.sigmoid(p[:, (1 + k) * d:(2 + k) * d])
        o_ref[:, (1 + k) * d:(2 + k) * d] = jnp.log2(f)
    o_ref[:, 3 * d:] = p[:, 3 * d:]


def _project(lay, kern, x, mod_all, norm_g, w, layer, j, extra, extra_specs, out_widths, out_dtype,
             name):
    d = x.shape[1]
    outs = pl.pallas_call(
        kern,
        grid=(lay.all_tiles,),
        in_specs=[lay.rows(d), lay.mod_spec(d, layer), _resident(norm_g.shape, (layer,)),
                  _resident(w.shape, (j,))] + extra_specs,
        out_specs=[lay.rows(wd) for wd in out_widths],
        out_shape=[jax.ShapeDtypeStruct((lay.n_all, wd), out_dtype) for wd in out_widths],
        compiler_params=_params("parallel"),
        name=name,
    )(x, mod_all, norm_g, w, *extra)
    return outs


def _nt_dot(a, b):
    return lax.dot_general(a, b, (((1,), (1,)), ((), ())), preferred_element_type=F32)


def _stage_keys(k_refs, v_refs, k_s, v_s):
    for h in range(k_s.shape[0]):
        sl = slice(h * HEAD_W, (h + 1) * HEAD_W)
        row = 0
        for k_ref, v_ref in zip(k_refs, v_refs):
            n = k_ref.shape[0]
            k_s[h, row:row + n, :] = k_ref[:, sl]
            v_s[h, row:row + n, 0:HEAD_W] = v_ref[:, sl]
            row += n
        v_s[h, :, HEAD_W:] = jnp.ones((v_s.shape[1], HEAD_W), BF16)


def _key_blocks(start, stop):
    return [(s, min(KEY_BLOCK, stop - s)) for s in range(start, stop, KEY_BLOCK)]


def _softmax_pv(queries, k_s, v_s, blocks, finish):
    steps = [(h, blk) for h in range(len(queries)) for blk in blocks]

    def scores(step):
        h, (start, size) = step
        return _nt_dot(queries[h], k_s[h, start:start + size, :])

    m = acc = None
    ahead = [scores(step) for step in steps[:SCORE_LOOKAHEAD]]
    for n, (h, (start, size)) in enumerate(steps):
        s = ahead.pop(0)
        if n + SCORE_LOOKAHEAD < len(steps):
            ahead.append(scores(steps[n + SCORE_LOOKAHEAD]))
        m_new = jnp.max(s, axis=-1, keepdims=True)
        if m is not None:
            m_new = jnp.maximum(m, m_new)
        e = jnp.exp2(s - m_new).astype(BF16)
        pv = jnp.dot(e, v_s[h, start:start + size, :], preferred_element_type=F32)
        acc = pv if acc is None else acc * jnp.exp2(m - m_new) + pv
        m = m_new
        if (start, size) == blocks[-1]:
            finish(h, acc[:, :HEAD_W] / acc[:, HEAD_W:])
            m = acc = None


def _da_attend(q_ref, k_s, v_s, blocks, lam, gain, o_ref):
    tq = q_ref.shape[0]
    lane = lax.broadcasted_iota(jnp.int32, (1, LANES), 1)
    first_map = (lane % (LANES // 2)) < (LANES // 4)
    queries = []
    for h in range(k_s.shape[0]):
        q = q_ref[:, h * HEAD_W:(h + 1) * HEAD_W]
        zero = jnp.zeros_like(q)
        queries.append(jnp.concatenate([jnp.where(first_map, q, zero),
                                        jnp.where(first_map, zero, q)], axis=0))

    def finish(h, p):
        o_ref[:, h * HEAD_W:(h + 1) * HEAD_W] = _rms(p[:tq] - lam * p[tq:], gain).astype(BF16)

    _softmax_pv(queries, k_s, v_s, blocks, finish)


def _split_refs(refs, n_seg, n_extra):
    keys, values = refs[1:1 + n_seg], refs[1 + n_seg:1 + 2 * n_seg]
    extra = refs[1 + 2 * n_seg:1 + 2 * n_seg + n_extra]
    return (refs[0], keys, values, extra) + tuple(refs[-3:])


def _da_attn_kernel(*refs, n_seg, lam_init):
    q_ref, k_refs, v_refs, (lam_ref, g_ref), o_ref, k_s, v_s = _split_refs(refs, n_seg, 2)
    lp = lam_ref[...]
    lam = (jnp.exp(jnp.sum(lp[0:1] * lp[1:2], axis=-1, keepdims=True))
           - jnp.exp(jnp.sum(lp[2:3] * lp[3:4], axis=-1, keepdims=True)) + lam_init)
    gain = g_ref[...] * (1.0 - lam_init)

    @pl.when(pl.program_id(2) == 0)
    def _():
        _stage_keys(k_refs, v_refs, k_s, v_s)

    _da_attend(q_ref, k_s, v_s, _key_blocks(0, k_s.shape[1]), lam, gain, o_ref)


def _gqa_attend(q_ref, k_s, v_s, blocks, o_ref):
    tq = q_ref.shape[0]
    group_w = GQA_GROUP * HEAD_W
    queries = [jnp.concatenate([q_ref[:, h * group_w + g * HEAD_W:h * group_w + (g + 1) * HEAD_W]
                                for g in range(GQA_GROUP)], axis=0) for h in range(k_s.shape[0])]

    def finish(h, o):
        for g in range(GQA_GROUP):
            lo = h * group_w + g * HEAD_W
            o_ref[:, lo:lo + HEAD_W] = o[g * tq:(g + 1) * tq].astype(BF16)

    _softmax_pv(queries, k_s, v_s, blocks, finish)


def _gqa_attn_kernel(*refs, n_seg):
    q_ref, k_refs, v_refs, _, o_ref, k_s, v_s = _split_refs(refs, n_seg, 0)

    @pl.when(pl.program_id(2) == 0)
    def _():
        _stage_keys(k_refs, v_refs, k_s, v_s)

    _gqa_attend(q_ref, k_s, v_s, _key_blocks(0, k_s.shape[1]), o_ref)


def _attention(lay, kern, q, k, v, extra, extra_specs, *, kv_heads, q_per_kv, tq, need_ctx, name):
    batch, seq, ctx, n_lat = lay.batch, lay.seq, lay.ctx, lay.n_lat
    hp = ATTN_KV_PER_STEP
    tc = min(tq, ctx)
    assert seq % tq == 0 and ctx % tc == 0 and n_lat % ctx == 0 and kv_heads % hp == 0
    q_block_w, kv_block_w = hp * q_per_kv * HEAD_W, hp * HEAD_W
    kv_lat = pl.BlockSpec((seq, kv_block_w), lambda b, h, i: (b, h))
    kv_ctx = pl.BlockSpec((ctx, kv_block_w), lambda b, h, i: (n_lat // ctx + b, h))

    def call(tile, tiles, first_tile, kv_specs, n_keys, call_name):
        n_seg = len(kv_specs)
        return pl.pallas_call(
            functools.partial(kern, n_seg=n_seg),
            grid=(batch, kv_heads // hp, tiles),
            in_specs=[pl.BlockSpec((tile, q_block_w), lambda b, h, i: (first_tile + b * tiles + i, h))]
                     + kv_specs + kv_specs + extra_specs,
            out_specs=pl.BlockSpec((tile, q_block_w), lambda b, h, i: (b * tiles + i, h)),
            out_shape=jax.ShapeDtypeStruct((batch * tiles * tile, q.shape[1]), BF16),
            scratch_shapes=[pltpu.VMEM((hp, n_keys, HEAD_W), BF16),
                            pltpu.VMEM((hp, n_keys, 2 * HEAD_W), BF16)],
            compiler_params=_params("parallel", "parallel", "arbitrary"),
            name=call_name,
        )(q, *[k] * n_seg, *[v] * n_seg, *extra)

    y_lat = call(tq, seq // tq, 0, [kv_lat, kv_ctx], seq + ctx, name)
    y_ctx = call(tc, ctx // tc, n_lat // tc, [kv_ctx], ctx, name + "_ctx") if need_ctx else None
    return y_lat, y_ctx


def _hg_direction(c, reverse):
    row = lax.broadcasted_iota(jnp.int32, (c, c), 0)
    col = lax.broadcasted_iota(jnp.int32, (c, c), 1)
    later, earlier = (col, row) if reverse else (row, col)
    tri = (earlier <= later).astype(F32)

    levels = []
    half = c // 2
    while half >= 1:
        same = (row // (2 * half)) == (col // (2 * half))
        q_side = (later % (2 * half)) >= half
        k_side = (earlier % (2 * half)) < half
        levels.append((half, same & q_side & k_side))
        half //= 2
    nv = c // SUBLANES
    t_loc = lax.broadcasted_iota(jnp.int32, (nv, SUBLANES, HEAD_W), 1)

    def boundary(cum, half):
        b_row = half if reverse else half - 1
        if 2 * half >= SUBLANES:
            c3 = cum.reshape(c // (2 * half), 2 * half, HEAD_W)
            return jnp.broadcast_to(c3[:, b_row:b_row + 1, :], c3.shape).reshape(c, HEAD_W)
        c3 = cum.reshape(nv, SUBLANES, HEAD_W)
        ref = None
        for blk in range(SUBLANES // (2 * half)):
            r = blk * 2 * half + b_row
            cand = jnp.broadcast_to(c3[:, r:r + 1, :], c3.shape)
            ref = cand if ref is None else jnp.where(t_loc >= blk * 2 * half, cand, ref)
        return ref.reshape(c, HEAD_W)

    return tri, row == col, levels, boundary, (0 if reverse else c - 1)


def _hg_scan_kernel(qf_ref, lff_ref, vf_ref, qb_ref, lfb_ref, vb_ref, of_ref, ob_ref, stf_ref, stb_ref):
    c = HG_CHUNK
    n_sub = qf_ref.shape[0] // c

    @pl.when(pl.program_id(1) == 0)
    def _():
        stf_ref[...] = jnp.zeros_like(stf_ref)
        stb_ref[...] = jnp.zeros_like(stb_ref)

    fwd, bwd = _hg_direction(c, False), _hg_direction(c, True)
    heads = range(qf_ref.shape[1] // HEAD_W)
    states_f = [stf_ref[h] for h in heads]
    states_b = [stb_ref[h] for h in heads]

    for sub in range(n_sub):
        rows = slice(sub * c, (sub + 1) * c)
        _hg_chunk(qf_ref, lff_ref, vf_ref, of_ref, rows, states_f, *fwd)
        rows = slice((n_sub - 1 - sub) * c, (n_sub - sub) * c)
        _hg_chunk(qb_ref, lfb_ref, vb_ref, ob_ref, rows, states_b, *bwd)

    for h in heads:
        stf_ref[h] = states_f[h]
        stb_ref[h] = states_b[h]


def _hg_chunk(q_ref, lf_ref, v_ref, o_ref, rows, states, tri, diagonal, levels, boundary, end):
    cum_all = jnp.dot(tri, lf_ref[rows, :], preferred_element_type=F32,
                      precision=lax.Precision.HIGHEST)
    staged = []
    for h in range(len(states)):
        sl = slice(h * HEAD_W, (h + 1) * HEAD_W)
        q, lf, v = q_ref[rows, sl], lf_ref[rows, sl], v_ref[rows, sl]
        cum = cum_all[:, sl]
        f = jnp.exp2(lf)
        k = 1.0 - f
        v_b, k_b = v.astype(BF16), k.astype(BF16)
        att = jnp.where(diagonal, _nt_dot(q.astype(BF16), k_b), 0.0)
        for half, mask in levels:
            if half == 1:
                qt, kt = (q * f).astype(BF16), k_b
            else:
                decay = jnp.exp2(-jnp.abs(cum - boundary(cum, half)))
                qt, kt = (q * decay).astype(BF16), (k * decay).astype(BF16)
            att = jnp.where(mask, _nt_dot(qt, kt), att)
        total = cum[end:end + 1, :]
        st = states[h]
        inter = _nt_dot((q * jnp.exp2(cum)).astype(BF16), st.astype(BF16))
        kd = (k * jnp.exp2(total - cum)).astype(BF16)
        upd = lax.dot_general(v_b, kd, (((0,), (0,)), ((), ())), preferred_element_type=F32)
        states[h] = st * jnp.exp2(total) + upd
        staged.append((sl, att.astype(BF16), v_b, inter))

    for sl, att_b, v_b, inter in staged:
        o_ref[rows, sl] = jnp.dot(att_b, v_b, preferred_element_type=F32) + inter


def _hg_scan(lay, p, d):
    batch, seq, ctx, n_lat = lay.batch, lay.seq, lay.ctx, lay.n_lat
    c = HG_CHUNK * HG_CHUNKS_PER_STEP
    assert seq % c == 0 and ctx % c == 0
    nlc, ncc = seq // c, ctx // c
    ctx_base = n_lat // c

    def blk(b, j, reverse):
        c_ctx = (ncc - 1 - j) if reverse else j
        c_lat = (nlc - 1 - (j - ncc)) if reverse else (j - ncc)
        return jnp.where(j < ncc, ctx_base + b * ncc + c_ctx, b * nlc + c_lat)

    def col_spec(colblk, reverse):
        return pl.BlockSpec((c, d), lambda b, j: (blk(b, j, reverse), colblk))

    state = pltpu.VMEM((d // HEAD_W, HEAD_W, HEAD_W), F32)
    return pl.pallas_call(
        _hg_scan_kernel,
        grid=(batch, ncc + nlc),
        in_specs=[col_spec(0, False), col_spec(1, False), col_spec(3, False),
                  col_spec(0, True), col_spec(2, True), col_spec(3, True)],
        out_specs=[col_spec(0, False), col_spec(0, True)],
        out_shape=[jax.ShapeDtypeStruct((lay.n_all, d), F32)] * 2,
        scratch_shapes=[state, state],
        compiler_params=_params("arbitrary", "arbitrary"),
        name="hg_scan",
    )(p, p, p, p, p, p)


def _post_mixer(x, y_b, mod_ref, g_ref, wo_ref, win_ref, wout_ref):
    y = jnp.dot(y_b, wo_ref[...], preferred_element_type=F32)
    x = x + _rms(y, mod_ref[5:6, :] * g_ref[3:4, :])
    return _ffn(x, mod_ref, g_ref, win_ref, wout_ref, 1)


def _post_kernel(x_ref, y_ref, mod_ref, g_ref, wo_ref, win_ref, wout_ref, o_ref):
    o_ref[...] = _post_mixer(x_ref[...], y_ref[...], mod_ref, g_ref, wo_ref, win_ref, wout_ref)


def _post_split_kernel(x_ref, yl_ref, yc_ref, mod_ref, g_ref, wo_ref, win_ref, wout_ref, o_ref,
                       *, lat_tiles):
    y = jnp.where(pl.program_id(0) < lat_tiles, yl_ref[...], yc_ref[...])
    o_ref[...] = _post_mixer(x_ref[...], y, mod_ref, g_ref, wo_ref, win_ref, wout_ref)


def _post_hg_kernel(x_ref, of_ref, ob_ref, gate_ref, hn_ref, mod_ref, g_ref, wo_ref, win_ref,
                    wout_ref, o_ref):
    o = of_ref[...] + ob_ref[...]
    gate = gate_ref[...]
    parts = []
    for h in range(o.shape[1] // HEAD_W):
        sl = slice(h * HEAD_W, (h + 1) * HEAD_W)
        parts.append((_rms(o[:, sl], hn_ref[...]) * _silu(gate[:, sl])).astype(BF16))
    o_ref[...] = _post_mixer(x_ref[...], jnp.concatenate(parts, axis=1), mod_ref, g_ref, wo_ref,
                             win_ref, wout_ref)


def _post_sublayer(lay, kern, x, ys, y_specs, extra, extra_specs, mod_all, norm_g, w_o, w_in, w_out,
                   layer, j, n_tiles, name):
    d = x.shape[1]
    return pl.pallas_call(
        kern,
        grid=(n_tiles,),
        in_specs=[lay.rows(d)] + y_specs + extra_specs
                 + [lay.mod_spec(d, layer), _resident(norm_g.shape, (layer,)),
                    _resident(w_o.shape, (j,)), _resident(w_in.shape, (layer, 1)),
                    _resident(w_out.shape, (layer, 1))],
        out_specs=lay.rows(d),
        out_shape=jax.ShapeDtypeStruct((n_tiles * lay.tm, d), F32),
        compiler_params=_params("parallel"),
        name=name,
    )(x, *ys, *extra, mod_all, norm_g, w_o, w_in, w_out)


def _rope_tables(rows, head_dim, n_id_rows, maps):
    pairs = head_dim // 4
    inv_freq = jnp.power(ROPE_THETA, -jnp.arange(pairs, dtype=F32) / pairs)
    r = jnp.repeat(jnp.arange(rows, dtype=F32), GRID_W)
    col = jnp.tile(jnp.arange(GRID_W, dtype=F32), rows)
    ang = jnp.concatenate([r[:, None] * inv_freq, col[:, None] * inv_freq], axis=-1)
    cos, sin = jnp.cos(ang), jnp.sin(ang)
    cos = jnp.tile(cos, (1, 2 * maps))
    sin = jnp.concatenate([-jnp.tile(sin, (1, maps)), jnp.tile(sin, (1, maps))], axis=-1)
    cos = jnp.concatenate([cos, jnp.ones((n_id_rows, LANES), F32)], axis=0)
    sin = jnp.concatenate([sin, jnp.zeros((n_id_rows, LANES), F32)], axis=0)
    return cos, sin


def _da_head_perm():
    p = np.arange(2)[:, None, None]
    m = np.arange(2)[None, :, None]
    j = np.arange(HEAD_W // 4)[None, None, :]
    return (m * (HEAD_W // 2) + 2 * j + p).reshape(-1)


def _gqa_head_perm():
    p = np.arange(2)[:, None]
    j = np.arange(HEAD_W // 2)[None, :]
    return (2 * j + p).reshape(-1)


def _permute_heads(n_blocks, perm, n_tail):
    idx = (np.arange(n_blocks)[:, None] * HEAD_W + perm[None, :]).reshape(-1)
    return np.concatenate([idx, n_blocks * HEAD_W + np.arange(n_tail)])


def kernel(x, c, ctx, c_ctx, w_mod, b_mod, norm_g, ffn_w_in, ffn_w_out, da_w_qkv, da_lambda, da_subln, da_w_o, hg_w_in, hg_lower_bound, hg_norm, hg_w_o, gqa_w_qkv, gqa_q_norm, gqa_k_norm, gqa_w_o):
    batch, seq, d = x.shape
    n_ctx = ctx.shape[1]
    depth = w_mod.shape[0]
    assert d == N_HEADS * HEAD_W
    lay = _Layout(batch, seq, n_ctx, TOKEN_TILE)
    lay_p = _Layout(batch, seq, n_ctx, PROJ_TILE)
    rows = seq // GRID_W

    pad = (-(batch + 1)) % SUBLANES
    c_rows = jnp.concatenate([c, c_ctx[None, :], jnp.zeros((pad, d), F32)], axis=0)
    mod_all = _modulation(c_rows, w_mod, b_mod)[:, :batch + 1].reshape(depth, batch + 1, N_MOD, d)

    w_in, w_out = ffn_w_in.astype(BF16), ffn_w_out.astype(BF16)
    da_cos, da_sin = _rope_tables(rows, HEAD_W // 2, lay_p.tm, 2)
    gqa_cos, gqa_sin = _rope_tables(rows, HEAD_W, lay_p.tm, 1)
    da_w = da_w_qkv.astype(BF16)[:, :, _permute_heads(2 * N_HEADS, _da_head_perm(), d)]
    gqa_perm = _gqa_head_perm()
    kvw = d // GQA_GROUP
    gqa_w = gqa_w_qkv.astype(BF16)[:, :, _permute_heads(N_HEADS + N_HEADS // GQA_GROUP, gqa_perm, kvw)]
    gqa_qn, gqa_kn = gqa_q_norm[:, None, gqa_perm], gqa_k_norm[:, None, gqa_perm]
    da_wo, hg_wo, gqa_wo = da_w_o.astype(BF16), hg_w_o.astype(BF16), gqa_w_o.astype(BF16)
    hg_w = hg_w_in.astype(BF16)
    lb_table = jnp.cumsum(jax.nn.softmax(hg_lower_bound.astype(F32), axis=1), axis=1)
    lb_table = jnp.swapaxes(lb_table - lb_table[:, :1], 0, 1)
    da_gain, hg_gain = da_subln[:, None, :], hg_norm[:, None, :]

    xs = [x.reshape(batch * seq, d), ctx.reshape(batch * n_ctx, d)]
    for i in range(depth):
        kind, j = i % N_MIXERS, i // N_MIXERS
        need_ctx = i < depth - 1
        xa = _ffn_sublayer(lay, xs, mod_all, norm_g, w_in, w_out, i)
        post = functools.partial(_post_sublayer, mod_all=mod_all, norm_g=norm_g, w_in=w_in,
                                 w_out=w_out, layer=i, j=j)

        def attention_post(ys, w_o, name):
            y_lat, y_ctx = ys
            if y_ctx is None:
                return post(lay, _post_kernel, xa, [y_lat], [lay.rows(d)], [], [], w_o=w_o,
                            n_tiles=lay.lat_tiles, name=name)
            return post(lay, functools.partial(_post_split_kernel, lat_tiles=lay.lat_tiles), xa,
                        [y_lat, y_ctx], lay.split_rows(d), [], [], w_o=w_o,
                        n_tiles=lay.all_tiles, name=name)

        if kind == 0:
            lam_init = 0.8 - 0.6 * math.exp(-0.3 * i)
            q, k, v = _project(
                lay_p, functools.partial(_proj_da_kernel, q_scale=LOG2_E * (HEAD_W // 2) ** -0.5),
                xa, mod_all, norm_g, da_w, i, j, [da_cos, da_sin],
                [lay_p.rope_spec(), lay_p.rope_spec()],
                [d, d, d], BF16, "proj_da")
            ys = _attention(
                lay, functools.partial(_da_attn_kernel, lam_init=lam_init), q, k, v,
                [da_lambda, da_gain],
                [_resident(da_lambda.shape, (j,)), _resident(da_gain.shape, (j,))],
                kv_heads=N_HEADS, q_per_kv=1, tq=DA_Q_TILE, need_ctx=need_ctx, name="attn_da")
            xa = attention_post(ys, da_wo, "post_da")
        elif kind == 1:
            p, = _project(lay_p, _proj_hg_kernel, xa, mod_all, norm_g, hg_w, i, j,
                          [lb_table], [_resident(lb_table.shape, (i,))], [5 * d], F32, "proj_hg")
            o_f, o_b = _hg_scan(lay, p, d)
            xa = post(lay_p, _post_hg_kernel, xa, [o_f, o_b, p],
                      [lay_p.rows(d), lay_p.rows(d), lay_p.rows(d, 4)],
                      [hg_gain], [_resident(hg_gain.shape, (j,))], w_o=hg_wo,
                      n_tiles=lay_p.all_tiles if need_ctx else lay_p.lat_tiles, name="post_hg")
        else:
            q, k, v = _project(
                lay_p, functools.partial(_proj_gqa_kernel, q_scale=LOG2_E * HEAD_W ** -0.5),
                xa, mod_all, norm_g, gqa_w, i, j,
                [gqa_cos, gqa_sin, gqa_qn, gqa_kn],
                [lay_p.rope_spec(), lay_p.rope_spec(), _resident(gqa_qn.shape, (j,)),
                 _resident(gqa_kn.shape, (j,))],
                [d, kvw, kvw], BF16, "proj_gqa")
            ys = _attention(lay, _gqa_attn_kernel, q, k, v, [], [],
                            kv_heads=N_HEADS // GQA_GROUP, q_per_kv=GQA_GROUP,
                            tq=GQA_Q_TILE, need_ctx=need_ctx, name="attn_gqa")
            xa = attention_post(ys, gqa_wo, "post_gqa")
        xs = [xa]
    return xa.reshape(batch, seq, d)
```

```python
import functools
import math

import numpy as np
import jax
import jax.numpy as jnp
from jax import lax
from jax.experimental import pallas as pl
from jax.experimental.pallas import tpu as pltpu

F32 = jnp.float32
BF16 = jnp.bfloat16

NORM_EPS = 1e-6
ROPE_THETA = 10000.0
GRID_W = 64
N_MOD = 9
N_MIXERS = 3
MACARON_WEIGHT = 0.5
LOG2_E = math.log2(math.e)
LANES = 128
SUBLANES = 8
MXU_W = 256
HEAD_W = 128
N_HEADS = 8
GQA_GROUP = 4
VMEM_LIMIT = 56 * 1024 * 1024

TOKEN_TILE = 1024
PROJ_TILE = 512
DA_Q_TILE = 512
GQA_Q_TILE = 256
KEY_BLOCK = 512
ATTN_KV_PER_STEP = 2
SCORE_LOOKAHEAD = 1
HG_CHUNK = 64
HG_CHUNKS_PER_STEP = 4


def _params(*sem):
    return pltpu.CompilerParams(dimension_semantics=sem, vmem_limit_bytes=VMEM_LIMIT)


def _resident(shape, lead=()):
    block = (None,) * len(lead) + tuple(shape[len(lead):])
    index = tuple(lead) + (0,) * (len(shape) - len(lead))
    return pl.BlockSpec(block, lambda *_: index, pipeline_mode=pl.Buffered(1))


def _rms(x, g):
    return x * lax.rsqrt(jnp.mean(x * x, axis=-1, keepdims=True) + NORM_EPS) * g


def _modulated(x, g, shift, scale):
    return _rms(x, g * (1.0 + scale)) + shift


def _silu(x):
    return x * jax.nn.sigmoid(x)


def _mod_kernel(c_ref, w_ref, b_ref, o_ref):
    s = _silu(c_ref[...])
    o_ref[0] = jnp.dot(s, w_ref[0], preferred_element_type=F32,
                       precision=lax.Precision.HIGHEST) + b_ref[0]


def _modulation(c_rows, w_mod, b_mod):
    depth, d, n = w_mod.shape
    rows = c_rows.shape[0]
    tn = n // 8
    return pl.pallas_call(
        _mod_kernel,
        grid=(depth, n // tn),
        in_specs=[pl.BlockSpec((rows, d), lambda i, j: (0, 0)),
                  pl.BlockSpec((1, d, tn), lambda i, j: (i, 0, j)),
                  pl.BlockSpec((1, 1, tn), lambda i, j: (i, 0, j))],
        out_specs=pl.BlockSpec((1, rows, tn), lambda i, j: (i, 0, j)),
        out_shape=jax.ShapeDtypeStruct((depth, rows, n), F32),
        compiler_params=_params("arbitrary", "arbitrary"),
        name="modulation",
    )(c_rows, w_mod, b_mod.reshape(depth, 1, n))


class _Layout:
    def __init__(self, batch, seq, ctx, tm=TOKEN_TILE):
        self.batch, self.seq, self.ctx = batch, seq, ctx
        self.n_lat = batch * seq
        self.n_all = self.n_lat + batch * ctx
        assert seq % tm == 0 and (batch * ctx) % tm == 0
        self.tm = tm
        self.lat_tiles = self.n_lat // tm
        self.all_tiles = self.n_all // tm
        self.tiles_per_sample = seq // tm

    def mod_spec(self, d, layer):
        lat_tiles, tps, batch = self.lat_tiles, self.tiles_per_sample, self.batch
        return pl.BlockSpec((None, None, N_MOD, d),
                            lambda t: (layer, jnp.where(t < lat_tiles, t // tps, batch), 0, 0))

    def rope_spec(self):
        lat_tiles, tps = self.lat_tiles, self.tiles_per_sample
        return pl.BlockSpec((self.tm, LANES),
                            lambda t: (jnp.where(t < lat_tiles, t % tps, tps), 0))

    def rows(self, width, col=0):
        return pl.BlockSpec((self.tm, width), lambda t: (t, col))

    def split_rows(self, width):
        lat_tiles, tm = self.lat_tiles, self.tm
        return [pl.BlockSpec((tm, width), lambda t: (jnp.minimum(t, lat_tiles - 1), 0)),
                pl.BlockSpec((tm, width), lambda t: (jnp.maximum(t - lat_tiles, 0), 0),
                             pipeline_mode=pl.Buffered(1))]


def _ffn_chunks(d_ff):
    assert d_ff % MXU_W == 0
    first = (d_ff // MXU_W + 1) // 2 * MXU_W
    return [(0, first), (first, d_ff - first)]


def _ffn(x, mod_ref, g_ref, win_ref, wout_ref, which):
    mod_base, g_base = 6 * which, 4 * which
    d_ff = wout_ref.shape[0]
    shift = mod_ref[mod_base:mod_base + 1, :]
    scale = mod_ref[mod_base + 1:mod_base + 2, :]
    gate = mod_ref[mod_base + 2:mod_base + 3, :]
    h = _modulated(x, g_ref[g_base:g_base + 1, :], shift, scale).astype(BF16)
    chunks = _ffn_chunks(d_ff)
    hidden = [(jnp.dot(h, win_ref[:, lo:lo + n], preferred_element_type=F32),
               jnp.dot(h, win_ref[:, d_ff + lo:d_ff + lo + n], preferred_element_type=F32))
              for lo, n in chunks]
    y = None
    for (lo, n), (gt, up) in zip(chunks, hidden):
        a = (_silu(gt) * up).astype(BF16)
        part = jnp.dot(a, wout_ref[lo:lo + n, :], preferred_element_type=F32)
        y = part if y is None else y + part
    return x + _rms(y, (MACARON_WEIGHT * gate) * g_ref[g_base + 1:g_base + 2, :])


def _ffn_kernel(x_ref, mod_ref, g_ref, win_ref, wout_ref, o_ref):
    o_ref[...] = _ffn(x_ref[...], mod_ref, g_ref, win_ref, wout_ref, 0)


def _ffn_first_kernel(xl_ref, xc_ref, mod_ref, g_ref, win_ref, wout_ref, o_ref, *, lat_tiles):
    x = jnp.where(pl.program_id(0) < lat_tiles, xl_ref[...], xc_ref[...])
    o_ref[...] = _ffn(x, mod_ref, g_ref, win_ref, wout_ref, 0)


def _ffn_sublayer(lay, xs, mod_all, norm_g, w_in, w_out, layer):
    d = xs[0].shape[1]
    if len(xs) == 1:
        kern, x_specs = _ffn_kernel, [lay.rows(d)]
    else:
        kern, x_specs = functools.partial(_ffn_first_kernel, lat_tiles=lay.lat_tiles), lay.split_rows(d)
    return pl.pallas_call(
        kern,
        grid=(lay.all_tiles,),
        in_specs=x_specs + [lay.mod_spec(d, layer), _resident(norm_g.shape, (layer,)),
                            _resident(w_in.shape, (layer, 0)), _resident(w_out.shape, (layer, 0))],
        out_specs=lay.rows(d),
        out_shape=jax.ShapeDtypeStruct((lay.n_all, d), F32),
        compiler_params=_params("parallel"),
        name="ffn0",
    )(*xs, mod_all, norm_g, w_in, w_out)


def _mixer_matmul(x_ref, mod_ref, g_ref, w_ref):
    h = _modulated(x_ref[...], g_ref[2:3, :], mod_ref[3:4, :], mod_ref[4:5, :]).astype(BF16)
    return jnp.dot(h, w_ref[...], preferred_element_type=F32)


def _rope(x, cos, sin):
    return x * cos + pltpu.roll(x, LANES // 2, 1) * sin


def _proj_da_kernel(x_ref, mod_ref, g_ref, w_ref, cos_ref, sin_ref, q_ref, k_ref, v_ref, *, q_scale):
    qkv = _mixer_matmul(x_ref, mod_ref, g_ref, w_ref)
    cos, sin = cos_ref[...], sin_ref[...]
    width = q_ref.shape[1]
    cos_q, sin_q = cos * q_scale, sin * q_scale
    for lo in range(0, width, HEAD_W):
        q_ref[:, lo:lo + HEAD_W] = _rope(qkv[:, lo:lo + HEAD_W], cos_q, sin_q).astype(BF16)
        k_ref[:, lo:lo + HEAD_W] = _rope(qkv[:, width + lo:width + lo + HEAD_W], cos, sin).astype(BF16)
    v_ref[...] = qkv[:, 2 * width:].astype(BF16)


def _head_rms(x, g, ones):
    ss = jnp.dot((x * x).astype(BF16), ones, preferred_element_type=F32)
    return x * lax.rsqrt(ss * (1.0 / HEAD_W) + NORM_EPS) * g


def _proj_gqa_kernel(x_ref, mod_ref, g_ref, w_ref, cos_ref, sin_ref, qn_ref, kn_ref,
                     q_ref, k_ref, v_ref, *, q_scale):
    qkv = _mixer_matmul(x_ref, mod_ref, g_ref, w_ref)
    cos, sin = cos_ref[...], sin_ref[...]
    qw, kw = q_ref.shape[1], k_ref.shape[1]
    ones = jnp.ones((HEAD_W, HEAD_W), BF16)
    q_gain = qn_ref[...] * q_scale
    for lo in range(0, qw, HEAD_W):
        xq = _head_rms(qkv[:, lo:lo + HEAD_W], q_gain, ones)
        q_ref[:, lo:lo + HEAD_W] = _rope(xq, cos, sin).astype(BF16)
    for lo in range(0, kw, HEAD_W):
        xk = _head_rms(qkv[:, qw + lo:qw + lo + HEAD_W], kn_ref[...], ones)
        k_ref[:, lo:lo + HEAD_W] = _rope(xk, cos, sin).astype(BF16)
    v_ref[...] = qkv[:, qw + kw:].astype(BF16)


def _proj_hg_kernel(x_ref, mod_ref, g_ref, w_ref, lb_ref, o_ref):
    p = _mixer_matmul(x_ref, mod_ref, g_ref, w_ref)
    d = x_ref.shape[1]
    o_ref[:, 0:d] = _silu(p[:, 0:d])
    for k in range(2):
        lb = lb_ref[k:k + 1, :]
        f = lb + (1.0 - lb) * jax.nn.sigmoid(p[:, (1 + k) * d:(2 + k) * d])
        o_ref[:, (1 + k) * d:(2 + k) * d] = jnp.log2(f)
    o_ref[:, 3 * d:] = p[:, 3 * d:]


def _project(lay, kern, x, mod_all, norm_g, w, layer, j, extra, extra_specs, out_widths, out_dtype,
             name):
    d = x.shape[1]
    outs = pl.pallas_call(
        kern,
        grid=(lay.all_tiles,),
        in_specs=[lay.rows(d), lay.mod_spec(d, layer), _resident(norm_g.shape, (layer,)),
                  _resident(w.shape, (j,))] + extra_specs,
        out_specs=[lay.rows(wd) for wd in out_widths],
        out_shape=[jax.ShapeDtypeStruct((lay.n_all, wd), out_dtype) for wd in out_widths],
        compiler_params=_params("parallel"),
        name=name,
    )(x, mod_all, norm_g, w, *extra)
    return outs


def _nt_dot(a, b):
    return lax.dot_general(a, b, (((1,), (1,)), ((), ())), preferred_element_type=F32)


class _KeySegments:
    def __init__(self, k_refs):
        self.refs = k_refs
        self.heads = k_refs[0].shape[1] // HEAD_W
        self.rows = sum(r.shape[0] for r in k_refs)

    def block(self, h, start, size):
        for ref in self.refs:
            if start < ref.shape[0]:
                assert start + size <= ref.shape[0]
                return ref[start:start + size, h * HEAD_W:(h + 1) * HEAD_W]
            start -= ref.shape[0]
        raise ValueError("key block past the last segment")


def _stage_values(v_refs, v_s):
    for h in range(v_s.shape[0]):
        sl = slice(h * HEAD_W, (h + 1) * HEAD_W)
        row = 0
        for v_ref in v_refs:
            n = v_ref.shape[0]
            v_s[h, row:row + n, 0:HEAD_W] = v_ref[:, sl]
            row += n
        v_s[h, :, HEAD_W:] = jnp.ones((v_s.shape[1], HEAD_W), BF16)


def _key_blocks(start, stop):
    return [(s, min(KEY_BLOCK, stop - s)) for s in range(start, stop, KEY_BLOCK)]


def _softmax_pv(queries, k_s, v_s, blocks, finish):
    steps = [(h, blk) for h in range(len(queries)) for blk in blocks]

    def scores(step):
        h, (start, size) = step
        return _nt_dot(queries[h], k_s.block(h, start, size))

    m = acc = None
    ahead = [scores(step) for step in steps[:SCORE_LOOKAHEAD]]
    for n, (h, (start, size)) in enumerate(steps):
        s = ahead.pop(0)
        if n + SCORE_LOOKAHEAD < len(steps):
            ahead.append(scores(steps[n + SCORE_LOOKAHEAD]))
        m_new = jnp.max(s, axis=-1, keepdims=True)
        if m is not None:
            m_new = jnp.maximum(m, m_new)
        e = jnp.exp2(s - m_new).astype(BF16)
        pv = jnp.dot(e, v_s[h, start:start + size, :], preferred_element_type=F32)
        acc = pv if acc is None else acc * jnp.exp2(m - m_new) + pv
        m = m_new
        if (start, size) == blocks[-1]:
            finish(h, acc[:, :HEAD_W] / acc[:, HEAD_W:])
            m = acc = None


def _da_attend(q_ref, k_s, v_s, blocks, lam, gain, o_ref):
    tq = q_ref.shape[0]
    lane = lax.broadcasted_iota(jnp.int32, (1, LANES), 1)
    first_map = (lane % (LANES // 2)) < (LANES // 4)
    queries = []
    for h in range(k_s.heads):
        q = q_ref[:, h * HEAD_W:(h + 1) * HEAD_W]
        zero = jnp.zeros_like(q)
        queries.append(jnp.concatenate([jnp.where(first_map, q, zero),
                                        jnp.where(first_map, zero, q)], axis=0))

    def finish(h, p):
        o_ref[:, h * HEAD_W:(h + 1) * HEAD_W] = _rms(p[:tq] - lam * p[tq:], gain).astype(BF16)

    _softmax_pv(queries, k_s, v_s, blocks, finish)


def _split_refs(refs, n_seg, n_extra):
    keys, values = refs[1:1 + n_seg], refs[1 + n_seg:1 + 2 * n_seg]
    extra = refs[1 + 2 * n_seg:1 + 2 * n_seg + n_extra]
    return (refs[0], _KeySegments(keys), values, extra) + tuple(refs[-2:])


def _da_attn_kernel(*refs, n_seg, lam_init):
    q_ref, k_s, v_refs, (lam_ref, g_ref), o_ref, v_s = _split_refs(refs, n_seg, 2)
    lp = lam_ref[...]
    lam = (jnp.exp(jnp.sum(lp[0:1] * lp[1:2], axis=-1, keepdims=True))
           - jnp.exp(jnp.sum(lp[2:3] * lp[3:4], axis=-1, keepdims=True)) + lam_init)
    gain = g_ref[...] * (1.0 - lam_init)

    @pl.when(pl.program_id(2) == 0)
    def _():
        _stage_values(v_refs, v_s)

    _da_attend(q_ref, k_s, v_s, _key_blocks(0, k_s.rows), lam, gain, o_ref)


def _gqa_attend(q_ref, k_s, v_s, blocks, o_ref):
    tq = q_ref.shape[0]
    group_w = GQA_GROUP * HEAD_W
    queries = [jnp.concatenate([q_ref[:, h * group_w + g * HEAD_W:h * group_w + (g + 1) * HEAD_W]
                                for g in range(GQA_GROUP)], axis=0) for h in range(k_s.heads)]

    def finish(h, o):
        for g in range(GQA_GROUP):
            lo = h * group_w + g * HEAD_W
            o_ref[:, lo:lo + HEAD_W] = o[g * tq:(g + 1) * tq].astype(BF16)

    _softmax_pv(queries, k_s, v_s, blocks, finish)


def _gqa_attn_kernel(*refs, n_seg):
    q_ref, k_s, v_refs, _, o_ref, v_s = _split_refs(refs, n_seg, 0)

    @pl.when(pl.program_id(2) == 0)
    def _():
        _stage_values(v_refs, v_s)

    _gqa_attend(q_ref, k_s, v_s, _key_blocks(0, k_s.rows), o_ref)


def _attention(lay, kern, q, k, v, extra, extra_specs, *, kv_heads, q_per_kv, tq, need_ctx, name):
    batch, seq, ctx, n_lat = lay.batch, lay.seq, lay.ctx, lay.n_lat
    hp = ATTN_KV_PER_STEP
    tc = min(tq, ctx)
    assert seq % tq == 0 and ctx % tc == 0 and n_lat % ctx == 0 and kv_heads % hp == 0
    q_block_w, kv_block_w = hp * q_per_kv * HEAD_W, hp * HEAD_W
    kv_lat = pl.BlockSpec((seq, kv_block_w), lambda b, h, i: (b, h))
    kv_ctx = pl.BlockSpec((ctx, kv_block_w), lambda b, h, i: (n_lat // ctx + b, h))

    def call(tile, tiles, first_tile, kv_specs, n_keys, call_name):
        n_seg = len(kv_specs)
        return pl.pallas_call(
            functools.partial(kern, n_seg=n_seg),
            grid=(batch, kv_heads // hp, tiles),
            in_specs=[pl.BlockSpec((tile, q_block_w), lambda b, h, i: (first_tile + b * tiles + i, h))]
                     + kv_specs + kv_specs + extra_specs,
            out_specs=pl.BlockSpec((tile, q_block_w), lambda b, h, i: (b * tiles + i, h)),
            out_shape=jax.ShapeDtypeStruct((batch * tiles * tile, q.shape[1]), BF16),
            scratch_shapes=[pltpu.VMEM((hp, n_keys, 2 * HEAD_W), BF16)],
            compiler_params=_params("parallel", "parallel", "arbitrary"),
            name=call_name,
        )(q, *[k] * n_seg, *[v] * n_seg, *extra)

    y_lat = call(tq, seq // tq, 0, [kv_lat, kv_ctx], seq + ctx, name)
    y_ctx = call(tc, ctx // tc, n_lat // tc, [kv_ctx], ctx, name + "_ctx") if need_ctx else None
    return y_lat, y_ctx


def _hg_direction(c, reverse):
    row = lax.broadcasted_iota(jnp.int32, (c, c), 0)
    col = lax.broadcasted_iota(jnp.int32, (c, c), 1)
    later, earlier = (col, row) if reverse else (row, col)
    tri = (earlier <= later).astype(F32)

    levels = []
    half = c // 2
    while half >= 1:
        same = (row // (2 * half)) == (col // (2 * half))
        q_side = (later % (2 * half)) >= half
        k_side = (earlier % (2 * half)) < half
        levels.append((half, same & q_side & k_side))
        half //= 2
    nv = c // SUBLANES
    t_loc = lax.broadcasted_iota(jnp.int32, (nv, SUBLANES, HEAD_W), 1)

    def boundary(cum, half):
        b_row = half if reverse else half - 1
        if 2 * half >= SUBLANES:
            c3 = cum.reshape(c // (2 * half), 2 * half, HEAD_W)
            return jnp.broadcast_to(c3[:, b_row:b_row + 1, :], c3.shape).reshape(c, HEAD_W)
        c3 = cum.reshape(nv, SUBLANES, HEAD_W)
        ref = None
        for blk in range(SUBLANES // (2 * half)):
            r = blk * 2 * half + b_row
            cand = jnp.broadcast_to(c3[:, r:r + 1, :], c3.shape)
            ref = cand if ref is None else jnp.where(t_loc >= blk * 2 * half, cand, ref)
        return ref.reshape(c, HEAD_W)

    return tri, row == col, levels, boundary, (0 if reverse else c - 1)


def _hg_scan_kernel(qf_ref, lff_ref, vf_ref, qb_ref, lfb_ref, vb_ref, of_ref, ob_ref, stf_ref, stb_ref):
    c = HG_CHUNK
    n_sub = qf_ref.shape[0] // c

    @pl.when(pl.program_id(1) == 0)
    def _():
        stf_ref[...] = jnp.zeros_like(stf_ref)
        stb_ref[...] = jnp.zeros_like(stb_ref)

    fwd, bwd = _hg_direction(c, False), _hg_direction(c, True)
    heads = range(qf_ref.shape[1] // HEAD_W)
    states_f = [stf_ref[h] for h in heads]
    states_b = [stb_ref[h] for h in heads]

    for sub in range(n_sub):
        rows = slice(sub * c, (sub + 1) * c)
        _hg_chunk(qf_ref, lff_ref, vf_ref, of_ref, rows, states_f, *fwd)
        rows = slice((n_sub - 1 - sub) * c, (n_sub - sub) * c)
        _hg_chunk(qb_ref, lfb_ref, vb_ref, ob_ref, rows, states_b, *bwd)

    for h in heads:
        stf_ref[h] = states_f[h]
        stb_ref[h] = states_b[h]


def _hg_chunk(q_ref, lf_ref, v_ref, o_ref, rows, states, tri, diagonal, levels, boundary, end):
    cum_all = jnp.dot(tri, lf_ref[rows, :], preferred_element_type=F32,
                      precision=lax.Precision.HIGHEST)
    staged = []
    for h in range(len(states)):
        sl = slice(h * HEAD_W, (h + 1) * HEAD_W)
        q, lf, v = q_ref[rows, sl], lf_ref[rows, sl], v_ref[rows, sl]
        cum = cum_all[:, sl]
        f = jnp.exp2(lf)
        k = 1.0 - f
        v_b, k_b = v.astype(BF16), k.astype(BF16)
        att = jnp.where(diagonal, _nt_dot(q.astype(BF16), k_b), 0.0)
        for half, mask in levels:
            if half == 1:
                qt, kt = (q * f).astype(BF16), k_b
            else:
                decay = jnp.exp2(-jnp.abs(cum - boundary(cum, half)))
                qt, kt = (q * decay).astype(BF16), (k * decay).astype(BF16)
            att = jnp.where(mask, _nt_dot(qt, kt), att)
        total = cum[end:end + 1, :]
        st = states[h]
        inter = _nt_dot((q * jnp.exp2(cum)).astype(BF16), st.astype(BF16))
        kd = (k * jnp.exp2(total - cum)).astype(BF16)
        upd = lax.dot_general(v_b, kd, (((0,), (0,)), ((), ())), preferred_element_type=F32)
        states[h] = st * jnp.exp2(total) + upd
        staged.append((sl, att.astype(BF16), v_b, inter))

    for sl, att_b, v_b, inter in staged:
        o_ref[rows, sl] = jnp.dot(att_b, v_b, preferred_element_type=F32) + inter


def _hg_scan(lay, p, d):
    batch, seq, ctx, n_lat = lay.batch, lay.seq, lay.ctx, lay.n_lat
    c = HG_CHUNK * HG_CHUNKS_PER_STEP
    assert seq % c == 0 and ctx % c == 0
    nlc, ncc = seq // c, ctx // c
    ctx_base = n_lat // c

    def blk(b, j, reverse):
        c_ctx = (ncc - 1 - j) if reverse else j
        c_lat = (nlc - 1 - (j - ncc)) if reverse else (j - ncc)
        return jnp.where(j < ncc, ctx_base + b * ncc + c_ctx, b * nlc + c_lat)

    def col_spec(colblk, reverse):
        return pl.BlockSpec((c, d), lambda b, j: (blk(b, j, reverse), colblk))

    state = pltpu.VMEM((d // HEAD_W, HEAD_W, HEAD_W), F32)
    return pl.pallas_call(
        _hg_scan_kernel,
        grid=(batch, ncc + nlc),
        in_specs=[col_spec(0, False), col_spec(1, False), col_spec(3, False),
                  col_spec(0, True), col_spec(2, True), col_spec(3, True)],
        out_specs=[col_spec(0, False), col_spec(0, True)],
        out_shape=[jax.ShapeDtypeStruct((lay.n_all, d), F32)] * 2,
        scratch_shapes=[state, state],
        compiler_params=_params("arbitrary", "arbitrary"),
        name="hg_scan",
    )(p, p, p, p, p, p)


def _post_mixer(x, y_b, mod_ref, g_ref, wo_ref, win_ref, wout_ref):
    y = jnp.dot(y_b, wo_ref[...], preferred_element_type=F32)
    x = x + _rms(y, mod_ref[5:6, :] * g_ref[3:4, :])
    return _ffn(x, mod_ref, g_ref, win_ref, wout_ref, 1)


def _post_kernel(x_ref, y_ref, mod_ref, g_ref, wo_ref, win_ref, wout_ref, o_ref):
    o_ref[...] = _post_mixer(x_ref[...], y_ref[...], mod_ref, g_ref, wo_ref, win_ref, wout_ref)


def _post_split_kernel(x_ref, yl_ref, yc_ref, mod_ref, g_ref, wo_ref, win_ref, wout_ref, o_ref,
                       *, lat_tiles):
    y = jnp.where(pl.program_id(0) < lat_tiles, yl_ref[...], yc_ref[...])
    o_ref[...] = _post_mixer(x_ref[...], y, mod_ref, g_ref, wo_ref, win_ref, wout_ref)


def _post_hg_kernel(x_ref, of_ref, ob_ref, gate_ref, hn_ref, mod_ref, g_ref, wo_ref, win_ref,
                    wout_ref, o_ref):
    o = of_ref[...] + ob_ref[...]
    gate = gate_ref[...]
    parts = []
    for h in range(o.shape[1] // HEAD_W):
        sl = slice(h * HEAD_W, (h + 1) * HEAD_W)
        parts.append((_rms(o[:, sl], hn_ref[...]) * _silu(gate[:, sl])).astype(BF16))
    o_ref[...] = _post_mixer(x_ref[...], jnp.concatenate(parts, axis=1), mod_ref, g_ref, wo_ref,
                             win_ref, wout_ref)


def _post_sublayer(lay, kern, x, ys, y_specs, extra, extra_specs, mod_all, norm_g, w_o, w_in, w_out,
                   layer, j, n_tiles, name):
    d = x.shape[1]
    return pl.pallas_call(
        kern,
        grid=(n_tiles,),
        in_specs=[lay.rows(d)] + y_specs + extra_specs
                 + [lay.mod_spec(d, layer), _resident(norm_g.shape, (layer,)),
                    _resident(w_o.shape, (j,)), _resident(w_in.shape, (layer, 1)),
                    _resident(w_out.shape, (layer, 1))],
        out_specs=lay.rows(d),
        out_shape=jax.ShapeDtypeStruct((n_tiles * lay.tm, d), F32),
        compiler_params=_params("parallel"),
        name=name,
    )(x, *ys, *extra, mod_all, norm_g, w_o, w_in, w_out)


def _rope_tables(rows, head_dim, n_id_rows, maps):
    pairs = head_dim // 4
    inv_freq = jnp.power(ROPE_THETA, -jnp.arange(pairs, dtype=F32) / pairs)
    r = jnp.repeat(jnp.arange(rows, dtype=F32), GRID_W)
    col = jnp.tile(jnp.arange(GRID_W, dtype=F32), rows)
    ang = jnp.concatenate([r[:, None] * inv_freq, col[:, None] * inv_freq], axis=-1)
    cos, sin = jnp.cos(ang), jnp.sin(ang)
    cos = jnp.tile(cos, (1, 2 * maps))
    sin = jnp.concatenate([-jnp.tile(sin, (1, maps)), jnp.tile(sin, (1, maps))], axis=-1)
    cos = jnp.concatenate([cos, jnp.ones((n_id_rows, LANES), F32)], axis=0)
    sin = jnp.concatenate([sin, jnp.zeros((n_id_rows, LANES), F32)], axis=0)
    return cos, sin


def _da_head_perm():
    p = np.arange(2)[:, None, None]
    m = np.arange(2)[None, :, None]
    j = np.arange(HEAD_W // 4)[None, None, :]
    return (m * (HEAD_W // 2) + 2 * j + p).reshape(-1)


def _gqa_head_perm():
    p = np.arange(2)[:, None]
    j = np.arange(HEAD_W // 2)[None, :]
    return (2 * j + p).reshape(-1)


def _permute_heads(n_blocks, perm, n_tail):
    idx = (np.arange(n_blocks)[:, None] * HEAD_W + perm[None, :]).reshape(-1)
    return np.concatenate([idx, n_blocks * HEAD_W + np.arange(n_tail)])


def kernel(x, c, ctx, c_ctx, w_mod, b_mod, norm_g, ffn_w_in, ffn_w_out, da_w_qkv, da_lambda, da_subln, da_w_o, hg_w_in, hg_lower_bound, hg_norm, hg_w_o, gqa_w_qkv, gqa_q_norm, gqa_k_norm, gqa_w_o):
    batch, seq, d = x.shape
    n_ctx = ctx.shape[1]
    depth = w_mod.shape[0]
    assert d == N_HEADS * HEAD_W
    lay = _Layout(batch, seq, n_ctx, TOKEN_TILE)
    lay_p = _Layout(batch, seq, n_ctx, PROJ_TILE)
    rows = seq // GRID_W

    pad = (-(batch + 1)) % SUBLANES
    c_rows = jnp.concatenate([c, c_ctx[None, :], jnp.zeros((pad, d), F32)], axis=0)
    mod_all = _modulation(c_rows, w_mod, b_mod)[:, :batch + 1].reshape(depth, batch + 1, N_MOD, d)

    w_in, w_out = ffn_w_in.astype(BF16), ffn_w_out.astype(BF16)
    da_cos, da_sin = _rope_tables(rows, HEAD_W // 2, lay_p.tm, 2)
    gqa_cos, gqa_sin = _rope_tables(rows, HEAD_W, lay_p.tm, 1)
    da_w = da_w_qkv.astype(BF16)[:, :, _permute_heads(2 * N_HEADS, _da_head_perm(), d)]
    gqa_perm = _gqa_head_perm()
    kvw = d // GQA_GROUP
    gqa_w = gqa_w_qkv.astype(BF16)[:, :, _permute_heads(N_HEADS + N_HEADS // GQA_GROUP, gqa_perm, kvw)]
    gqa_qn, gqa_kn = gqa_q_norm[:, None, gqa_perm], gqa_k_norm[:, None, gqa_perm]
    da_wo, hg_wo, gqa_wo = da_w_o.astype(BF16), hg_w_o.astype(BF16), gqa_w_o.astype(BF16)
    hg_w = hg_w_in.astype(BF16)
    lb_table = jnp.cumsum(jax.nn.softmax(hg_lower_bound.astype(F32), axis=1), axis=1)
    lb_table = jnp.swapaxes(lb_table - lb_table[:, :1], 0, 1)
    da_gain, hg_gain = da_subln[:, None, :], hg_norm[:, None, :]

    xs = [x.reshape(batch * seq, d), ctx.reshape(batch * n_ctx, d)]
    for i in range(depth):
        kind, j = i % N_MIXERS, i // N_MIXERS
        need_ctx = i < depth - 1
        xa = _ffn_sublayer(lay, xs, mod_all, norm_g, w_in, w_out, i)
        post = functools.partial(_post_sublayer, mod_all=mod_all, norm_g=norm_g, w_in=w_in,
                                 w_out=w_out, layer=i, j=j)

        def attention_post(ys, w_o, name):
            y_lat, y_ctx = ys
            if y_ctx is None:
                return post(lay, _post_kernel, xa, [y_lat], [lay.rows(d)], [], [], w_o=w_o,
                            n_tiles=lay.lat_tiles, name=name)
            return post(lay, functools.partial(_post_split_kernel, lat_tiles=lay.lat_tiles), xa,
                        [y_lat, y_ctx], lay.split_rows(d), [], [], w_o=w_o,
                        n_tiles=lay.all_tiles, name=name)

        if kind == 0:
            lam_init = 0.8 - 0.6 * math.exp(-0.3 * i)
            q, k, v = _project(
                lay_p, functools.partial(_proj_da_kernel, q_scale=LOG2_E * (HEAD_W // 2) ** -0.5),
                xa, mod_all, norm_g, da_w, i, j, [da_cos, da_sin],
                [lay_p.rope_spec(), lay_p.rope_spec()],
                [d, d, d], BF16, "proj_da")
            ys = _attention(
                lay, functools.partial(_da_attn_kernel, lam_init=lam_init), q, k, v,
                [da_lambda, da_gain],
                [_resident(da_lambda.shape, (j,)), _resident(da_gain.shape, (j,))],
                kv_heads=N_HEADS, q_per_kv=1, tq=DA_Q_TILE, need_ctx=need_ctx, name="attn_da")
            xa = attention_post(ys, da_wo, "post_da")
        elif kind == 1:
            p, = _project(lay_p, _proj_hg_kernel, xa, mod_all, norm_g, hg_w, i, j,
                          [lb_table], [_resident(lb_table.shape, (i,))], [5 * d], F32, "proj_hg")
            o_f, o_b = _hg_scan(lay, p, d)
            xa = post(lay_p, _post_hg_kernel, xa, [o_f, o_b, p],
                      [lay_p.rows(d), lay_p.rows(d), lay_p.rows(d, 4)],
                      [hg_gain], [_resident(hg_gain.shape, (j,))], w_o=hg_wo,
                      n_tiles=lay_p.all_tiles if need_ctx else lay_p.lat_tiles, name="post_hg")
        else:
            q, k, v = _project(
                lay_p, functools.partial(_proj_gqa_kernel, q_scale=LOG2_E * HEAD_W ** -0.5),
                xa, mod_all, norm_g, gqa_w, i, j,
                [gqa_cos, gqa_sin, gqa_qn, gqa_kn],
                [lay_p.rope_spec(), lay_p.rope_spec(), _resident(gqa_qn.shape, (j,)),
                 _resident(gqa_kn.shape, (j,))],
                [d, kvw, kvw], BF16, "proj_gqa")
            ys = _attention(lay, _gqa_attn_kernel, q, k, v, [], [],
                            kv_heads=N_HEADS // GQA_GROUP, q_per_kv=GQA_GROUP,
                            tq=GQA_Q_TILE, need_ctx=need_ctx, name="attn_gqa")
            xa = attention_post(ys, gqa_wo, "post_gqa")
        xs = [xa]
    return xa.reshape(batch, seq, d)
```
